```python
import math
import jax, jax.numpy as jnp
from jax import lax
import numpy as np

D_MODEL = 1024
BATCH = 16
SEQ = 2048
DEPTH = 1

CHUNK = 64
N_MEM = 256
EPS = 1e-6
ROPE_THETA = 10000.0
NEG = -1e30

A_HEADS = 4
A_DK = 64
A_DV = 2 * A_DK
A_QK_W = A_HEADS * 2 * A_DK
A_WIDTH = A_HEADS * A_DV
Q_BLOCK = 128

B_HEADS = 8
B_DH = 64
B_WIDTH = B_HEADS * B_DH
B_LEFT_CHUNKS = 8
B_MAX_REL = 256

X_HEADS = 4
X_DH = D_MODEL // X_HEADS

D_FF = -(-(8 * D_MODEL) // (3 * 256)) * 256

OFF_QA = 0
OFF_KA = OFF_QA + A_QK_W
OFF_VA = OFF_KA + A_QK_W
OFF_QB = OFF_VA + A_WIDTH
OFF_KB = OFF_QB + B_WIDTH
OFF_VB = OFF_KB + B_WIDTH
OFF_GA = OFF_VB + B_WIDTH
OFF_GB = OFF_GA + D_MODEL
IN_COLS = OFF_GB + D_MODEL

kernel_name = "hybrid_diffattn_chunkband_memxattn_swiglu"


def rmsnorm(x, g):
    xf = x.astype(jnp.float32)
    y = xf * lax.rsqrt(jnp.mean(xf * xf, axis=-1, keepdims=True) + EPS)
    return (y * g.astype(jnp.float32)).astype(x.dtype)


def rope_tables(seq, dim):
    inv = 1.0 / (ROPE_THETA ** (jnp.arange(0, dim, 2, dtype=jnp.float32) / dim))
    ang = jnp.arange(seq, dtype=jnp.float32)[:, None] * inv[None, :]
    return jnp.cos(ang), jnp.sin(ang)


def apply_rope(x, cos, sin):
    x1, x2 = jnp.split(x.astype(jnp.float32), 2, axis=-1)
    out = jnp.concatenate([x1 * cos - x2 * sin, x2 * cos + x1 * sin], axis=-1)
    return out.astype(x.dtype)


def diff_attention(q, k, v, lam, subln_g, lam_init):
    b, s = q.shape[0], q.shape[1]
    nqb = s // Q_BLOCK
    scale = A_DK ** -0.5
    k_chunk = jnp.arange(s) // CHUNK
    qb = q.reshape(b, nqb, Q_BLOCK, A_HEADS, 2, A_DK).transpose(1, 0, 2, 3, 4, 5)

    def one_block(args):
        qi, blk = args
        sc = jnp.einsum('bqhcd,bkhcd->bchqk', qi, k).astype(jnp.float32) * scale
        q_chunk = (blk * Q_BLOCK + jnp.arange(Q_BLOCK)) // CHUNK
        mask = k_chunk[None, :] <= q_chunk[:, None]
        sc = jnp.where(mask[None, None, None], sc, NEG)
        p = jax.nn.softmax(sc, axis=-1)
        attn = p[:, 0] - lam * p[:, 1]
        return jnp.einsum('bhqk,bkhd->bqhd', attn.astype(v.dtype), v)

    o = lax.map(one_block, (qb, jnp.arange(nqb)))
    o = o.transpose(1, 0, 2, 3, 4).reshape(b, s, A_HEADS, A_DV)
    o = rmsnorm(o, subln_g) * (1.0 - lam_init)
    return o.reshape(b, s, A_WIDTH)


def chunk_band_attention(q, k, v, rel_bias):
    b, s = q.shape[0], q.shape[1]
    nc = s // CHUNK
    pad = B_LEFT_CHUNKS * CHUNK
    band = pad + CHUNK
    scale = B_DH ** -0.5
    kp = jnp.pad(k, ((0, 0), (pad, 0), (0, 0), (0, 0)))
    vp = jnp.pad(v, ((0, 0), (pad, 0), (0, 0), (0, 0)))
    i = jnp.arange(CHUNK)
    j = jnp.arange(band)
    dist = i[:, None] + pad - j[None, :]
    idx = jnp.clip(dist, -B_MAX_REL, B_MAX_REL) + B_MAX_REL
    bias = rel_bias[:, idx].astype(jnp.float32)
    qc = q.reshape(b, nc, CHUNK, B_HEADS, B_DH).transpose(1, 0, 2, 3, 4)

    def one_chunk(args):
        qi, c = args
        start = c * CHUNK
        kb = lax.dynamic_slice_in_dim(kp, start, band, axis=1)
        vb = lax.dynamic_slice_in_dim(vp, start, band, axis=1)
        sc = jnp.einsum('bqhd,bkhd->bhqk', qi, kb).astype(jnp.float32) * scale + bias[None]
        valid = (start + j) >= pad
        sc = jnp.where(valid[None, None, None, :], sc, NEG)
        p = jax.nn.softmax(sc, axis=-1)
        return jnp.einsum('bhqk,bkhd->bqhd', p.astype(vb.dtype), vb)

    o = lax.map(one_chunk, (qc, jnp.arange(nc)))
    return o.transpose(1, 0, 2, 3, 4).reshape(b, s, B_WIDTH)


def memory_cross_attention(h, m, w_cq, w_ckv, w_co):
    b, s = h.shape[0], h.shape[1]
    nm = m.shape[1]
    q = (h @ w_cq).reshape(b, s, X_HEADS, X_DH)
    k, v = jnp.split(m @ w_ckv, 2, axis=-1)
    k = k.reshape(b, nm, X_HEADS, X_DH)
    v = v.reshape(b, nm, X_HEADS, X_DH)
    sc = jnp.einsum('bshd,bmhd->bhsm', q, k).astype(jnp.float32) * (X_DH ** -0.5)
    p = jax.nn.softmax(sc, axis=-1)
    o = jnp.einsum('bhsm,bmhd->bshd', p.astype(v.dtype), v).reshape(b, s, D_MODEL)
    return o @ w_co


def swiglu(h, w_gate_up, w_down):
    g, u = jnp.split(h @ w_gate_up, 2, axis=-1)
    return (jax.nn.silu(g) * u) @ w_down


def setup_inputs(seed: int = 0) -> dict:
    key = jax.random.key(seed)
    ks = jax.random.split(key, 24)
    nrm = lambda k, shape, sc: jax.random.normal(k, shape, jnp.float32) * sc
    gain = lambda k, shape: 1.0 + 0.05 * jax.random.normal(k, shape, jnp.float32)
    L = DEPTH
    return {
        "x": nrm(ks[0], (BATCH, SEQ, D_MODEL), 1.0),
        "mem": nrm(ks[1], (BATCH, N_MEM, D_MODEL), 1.0),
        "norm_mix_g": gain(ks[2], (L, D_MODEL)),
        "w_in": nrm(ks[3], (L, D_MODEL, IN_COLS), D_MODEL ** -0.5),
        "lam_q1": nrm(ks[4], (L, A_DK), 0.1),
        "lam_k1": nrm(ks[5], (L, A_DK), 0.1),
        "lam_q2": nrm(ks[6], (L, A_DK), 0.1),
        "lam_k2": nrm(ks[7], (L, A_DK), 0.1),
        "subln_g": gain(ks[8], (L, A_DV)),
        "rel_bias": nrm(ks[9], (L, B_HEADS, 2 * B_MAX_REL + 1), 0.1),
        "w_up_a": nrm(ks[10], (L, A_WIDTH, D_MODEL), A_WIDTH ** -0.5),
        "w_up_b": nrm(ks[11], (L, B_WIDTH, D_MODEL), B_WIDTH ** -0.5),
        "w_out": nrm(ks[12], (L, D_MODEL, D_MODEL), D_MODEL ** -0.5),
        "norm_cross_g": gain(ks[13], (L, D_MODEL)),
        "norm_mem_g": gain(ks[14], (L, D_MODEL)),
        "w_cq": nrm(ks[15], (L, D_MODEL, D_MODEL), D_MODEL ** -0.5),
        "w_ckv": nrm(ks[16], (L, D_MODEL, 2 * D_MODEL), D_MODEL ** -0.5),
        "w_co": nrm(ks[17], (L, D_MODEL, D_MODEL), D_MODEL ** -0.5),
        "norm_ffn_g": gain(ks[18], (L, D_MODEL)),
        "w_gate_up": nrm(ks[19], (L, D_MODEL, 2 * D_FF), D_MODEL ** -0.5),
        "w_down": nrm(ks[20], (L, D_FF, D_MODEL), D_FF ** -0.5),
        "norm_final_g": gain(ks[21], (D_MODEL,)),
    }


def reference(x, mem, norm_mix_g, w_in, lam_q1, lam_k1, lam_q2, lam_k2, subln_g,
              rel_bias, w_up_a, w_up_b, w_out, norm_cross_g, norm_mem_g, w_cq,
              w_ckv, w_co, norm_ffn_g, w_gate_up, w_down, norm_final_g):
    b, s = x.shape[0], x.shape[1]
    cos, sin = rope_tables(s, A_DK)
    cos_a = cos[None, :, None, None, :]
    sin_a = sin[None, :, None, None, :]
    for layer in range(DEPTH):
        lam_init = 0.8 - 0.6 * math.exp(-0.3 * layer)
        h = rmsnorm(x, norm_mix_g[layer])
        z = h @ w_in[layer]
        qa = z[..., OFF_QA:OFF_KA].reshape(b, s, A_HEADS, 2, A_DK)
        ka = z[..., OFF_KA:OFF_VA].reshape(b, s, A_HEADS, 2, A_DK)
        va = z[..., OFF_VA:OFF_QB].reshape(b, s, A_HEADS, A_DV)
        qb = z[..., OFF_QB:OFF_KB].reshape(b, s, B_HEADS, B_DH)
        kb = z[..., OFF_KB:OFF_VB].reshape(b, s, B_HEADS, B_DH)
        vb = z[..., OFF_VB:OFF_GA].reshape(b, s, B_HEADS, B_DH)
        ga = z[..., OFF_GA:OFF_GB]
        gb = z[..., OFF_GB:IN_COLS]
        qa = apply_rope(qa, cos_a, sin_a)
        ka = apply_rope(ka, cos_a, sin_a)
        lam = (jnp.exp(jnp.sum(lam_q1[layer].astype(jnp.float32) * lam_k1[layer].astype(jnp.float32)))
               - jnp.exp(jnp.sum(lam_q2[layer].astype(jnp.float32) * lam_k2[layer].astype(jnp.float32)))
               + lam_init)
        ya = diff_attention(qa, ka, va, lam, subln_g[layer], lam_init)
        yb = chunk_band_attention(qb, kb, vb, rel_bias[layer])
        merged = (jax.nn.sigmoid(ga) * (ya @ w_up_a[layer])
                  + jax.nn.sigmoid(gb) * (yb @ w_up_b[layer]))
        x = x + merged @ w_out[layer]
        hc = rmsnorm(x, norm_cross_g[layer])
        mn = rmsnorm(mem, norm_mem_g[layer])
        x = x + memory_cross_attention(hc, mn, w_cq[layer], w_ckv[layer], w_co[layer])
        hf = rmsnorm(x, norm_ffn_g[layer])
        x = x + swiglu(hf, w_gate_up[layer], w_down[layer])
    return rmsnorm(x, norm_final_g)
```

```python
import functools
import math

import jax
import jax.numpy as jnp
from jax import lax
from jax.experimental import pallas as pl
from jax.experimental.pallas import tpu as pltpu

CHUNK = 64
EPS = 1e-6
ROPE_THETA = 10000.0
NEG = -1e30

A_HEADS = 4
A_DK = 64
A_DV = 2 * A_DK
B_HEADS = 8
B_DH = 64
B_LEFT_CHUNKS = 8
B_MAX_REL = 256
X_HEADS = 4

LANES_V7X = 128
VMEM_LIMIT_BYTES_V7X = 56 * 1024 * 1024

SEQ_TILE = 512
ATT_BLOCK = 256
FF_CHUNK = 512

BF16 = jnp.bfloat16
F32 = jnp.float32


def _rms(x, g):
    return x * lax.rsqrt(jnp.mean(x * x, axis=-1, keepdims=True) + EPS) * g


def _dot(a, b):
    return jnp.dot(a, b, preferred_element_type=F32)


def _dot_nt(a, b):
    return lax.dot_general(a, b, (((1,), (1,)), ((), ())), preferred_element_type=F32)


def _sigmoid(x):
    return 1.0 / (1.0 + jnp.exp(-x))


def _const_spec(shape):
    n = len(shape)
    return pl.BlockSpec(shape, lambda *_: (0,) * n, pipeline_mode=pl.Buffered(1))


def _params(semantics):
    return pltpu.CompilerParams(dimension_semantics=semantics,
                                vmem_limit_bytes=VMEM_LIMIT_BYTES_V7X)


def _mem_kv_kernel(mem_ref, g_ref, w_ref, k_ref, v_ref):
    nb, nm, d = mem_ref.shape
    m = mem_ref[...].reshape(nb * nm, d)
    mn = _rms(m, g_ref[...]).astype(BF16)
    kv = _dot(mn, w_ref[...])
    k_ref[...] = kv[:, :d].astype(BF16).reshape(nb, nm, d)
    v_ref[...] = kv[:, d:].astype(BF16).reshape(nb, nm, d)


def _mem_kv(mem, g, w_ckv):
    b, nm, d = mem.shape
    nb = 2 if b % 2 == 0 else 1
    return pl.pallas_call(
        _mem_kv_kernel,
        grid=(b // nb,),
        in_specs=[pl.BlockSpec((nb, nm, d), lambda i: (i, 0, 0)),
                  _const_spec((1, d)),
                  _const_spec((d, 2 * d))],
        out_specs=[pl.BlockSpec((nb, nm, d), lambda i: (i, 0, 0))] * 2,
        out_shape=[jax.ShapeDtypeStruct((b, nm, d), BF16)] * 2,
        compiler_params=_params(("arbitrary",)),
        name="mem_kv",
    )(mem, g, w_ckv)


def _in_proj_kernel(x_ref, g_ref, w_ref, cos_ref, sin_ref, qkv_ref, gate_ref, *, q_scale):
    ts = x_ref.shape[1]
    h = _rms(x_ref[0], g_ref[...]).astype(BF16)
    cos = cos_ref[...]
    sin = sin_ref[...]
    lane = lax.broadcasted_iota(jnp.int32, (ts, LANES_V7X), 1)
    first_half = (lane % A_DK) < (A_DK // 2)
    slab_w = LANES_V7X
    grp_w = 4 * slab_w
    for grp in range(6):
        z = _dot(h, w_ref[:, grp * grp_w:(grp + 1) * grp_w])
        for s in range(4):
            zs = z[:, s * slab_w:(s + 1) * slab_w]
            if grp < 2:
                partner = jnp.where(first_half,
                                    pltpu.roll(zs, LANES_V7X - A_DK // 2, 1),
                                    pltpu.roll(zs, A_DK // 2, 1))
                zs = zs * cos + partner * sin
            if grp in (0, 3):
                zs = zs * q_scale
            qkv_ref[0, grp * 4 + s] = zs.astype(BF16)
    gate0 = 6 * grp_w
    for c in range(gate_ref.shape[2] // grp_w):
        z = _dot(h, w_ref[:, gate0 + c * grp_w: gate0 + (c + 1) * grp_w])
        gate_ref[0, :, c * grp_w:(c + 1) * grp_w] = _sigmoid(z).astype(BF16)


def _in_proj(x, g, w_in, cos_t, sin_t):
    b, s, d = x.shape
    ts = SEQ_TILE
    n_gate = w_in.shape[1] - 6 * 4 * LANES_V7X
    kern = functools.partial(_in_proj_kernel, q_scale=A_DK ** -0.5)
    return pl.pallas_call(
        kern,
        grid=(b, s // ts),
        in_specs=[pl.BlockSpec((1, ts, d), lambda bi, i: (bi, i, 0)),
                  _const_spec((1, d)),
                  _const_spec(w_in.shape),
                  pl.BlockSpec((ts, LANES_V7X), lambda bi, i: (i, 0)),
                  pl.BlockSpec((ts, LANES_V7X), lambda bi, i: (i, 0))],
        out_specs=[pl.BlockSpec((1, 24, ts, LANES_V7X), lambda bi, i: (bi, 0, i, 0)),
                   pl.BlockSpec((1, ts, n_gate), lambda bi, i: (bi, i, 0))],
        out_shape=[jax.ShapeDtypeStruct((b, 24, s, LANES_V7X), BF16),
                   jax.ShapeDtypeStruct((b, s, n_gate), BF16)],
        compiler_params=_params(("arbitrary", "arbitrary")),
        name="in_proj",
    )(x, g, w_in, cos_t, sin_t)


def _softmax_block_update(s, v, m_ref, l_ref, acc_ref, first):
    tk = s.shape[1]
    m_cur = jnp.max(s, axis=1, keepdims=True)
    if first:
        m_new = jnp.broadcast_to(m_cur, m_ref.shape)
    else:
        m_prev = m_ref[...]
        m_new = jnp.maximum(m_prev, m_cur)
    p = jnp.exp(s - jnp.concatenate([m_new] * (tk // LANES_V7X), axis=1))
    l_cur = jnp.sum(p, axis=1, keepdims=True)
    pv = _dot(p.astype(BF16), v)
    if first:
        l_ref[...] = jnp.broadcast_to(l_cur, l_ref.shape)
        acc_ref[...] = pv
    else:
        alpha = jnp.exp(m_prev - m_new)
        l_ref[...] = alpha * l_ref[...] + l_cur
        acc_ref[...] = alpha * acc_ref[...] + pv
    m_ref[...] = m_new


def _diff_attn_kernel(q_ref, k_ref, v_ref, lq1_ref, lk1_ref, lq2_ref, lk2_ref, g_ref,
                      o_ref, m_ref, l_ref, acc_ref, *, lam_init):
    tq = q_ref.shape[2]
    qi = pl.program_id(2)
    q = q_ref[0, 0]
    lane = lax.broadcasted_iota(jnp.int32, q.shape, 1)
    zero = jnp.zeros_like(q)
    qq = jnp.concatenate([jnp.where(lane < A_DK, q, zero),
                          jnp.where(lane >= A_DK, q, zero)], axis=0)

    k0 = pl.multiple_of(qi * tq, tq)
    s = _dot_nt(qq, k_ref[0, 0, pl.ds(k0, tq), :])
    row = lax.broadcasted_iota(jnp.int32, s.shape, 0)
    col = lax.broadcasted_iota(jnp.int32, s.shape, 1)
    allowed = (col // CHUNK) <= ((row % tq) // CHUNK)
    s = jnp.where(allowed, s, NEG)
    _softmax_block_update(s, v_ref[0, 0, pl.ds(k0, tq), :], m_ref, l_ref, acc_ref, True)

    def body(kj, carry):
        kk = pl.multiple_of(kj * tq, tq)
        sj = _dot_nt(qq, k_ref[0, 0, pl.ds(kk, tq), :])
        _softmax_block_update(sj, v_ref[0, 0, pl.ds(kk, tq), :], m_ref, l_ref, acc_ref, False)
        return carry

    lax.fori_loop(0, qi, body, 0)

    lam = (jnp.exp(jnp.sum(lq1_ref[...] * lk1_ref[...], axis=1, keepdims=True))
           - jnp.exp(jnp.sum(lq2_ref[...] * lk2_ref[...], axis=1, keepdims=True))
           + lam_init)
    o = acc_ref[...] / l_ref[...]
    o = o[:tq] - lam * o[tq:]
    o = _rms(o, g_ref[...]) * (1.0 - lam_init)
    o_ref[0] = o.astype(BF16)


def _diff_attn(qkv, lq1, lk1, lq2, lk2, subln_g, lam_init):
    b, _, s, w = qkv.shape
    tq = ATT_BLOCK
    kern = functools.partial(_diff_attn_kernel, lam_init=lam_init)
    vec = _const_spec((1, A_DK))
    return pl.pallas_call(
        kern,
        grid=(b, A_HEADS, s // tq),
        in_specs=[pl.BlockSpec((1, 1, tq, w), lambda bi, h, i: (bi, h, i, 0)),
                  pl.BlockSpec((1, 1, s, w), lambda bi, h, i: (bi, 4 + h, 0, 0)),
                  pl.BlockSpec((1, 1, s, w), lambda bi, h, i: (bi, 8 + h, 0, 0)),
                  vec, vec, vec, vec,
                  _const_spec((1, A_DV))],
        out_specs=pl.BlockSpec((1, tq, w), lambda bi, h, i: (bi, i, h)),
        out_shape=jax.ShapeDtypeStruct((b, s, A_HEADS * A_DV), BF16),
        scratch_shapes=[pltpu.VMEM((2 * tq, w), F32)] * 3,
        compiler_params=_params(("arbitrary", "arbitrary", "arbitrary")),
        name="diff_attn",
    )(qkv, qkv, qkv, lq1, lk1, lq2, lk2, subln_g)


def _band_attn_kernel(q_ref, k_ref, v_ref, tab_ref, o_ref):
    tq = q_ref.shape[2]
    qi = pl.program_id(2)
    q = q_ref[0, 0]
    lane = lax.broadcasted_iota(jnp.int32, q.shape, 1)
    zero = jnp.zeros_like(q)
    qq = jnp.concatenate([jnp.where(lane < B_DH, q, zero),
                          jnp.where(lane >= B_DH, q, zero)], axis=0)

    blocks = []
    for back in range(3):
        kb = jnp.maximum(qi - back, 0)
        tile = jnp.where(qi >= back, back, 3)
        kk = pl.multiple_of(kb * tq, tq)
        s = _dot_nt(qq, k_ref[0, 0, pl.ds(kk, tq), :]) + tab_ref[0, tile]
        blocks.append((s, kk))
    m = functools.reduce(jnp.maximum, [jnp.max(s, axis=1, keepdims=True) for s, _ in blocks])
    l = jnp.zeros_like(m)
    acc = jnp.zeros((2 * tq, LANES_V7X), F32)
    for s, kk in blocks:
        p = jnp.exp(s - m)
        l = l + jnp.sum(p, axis=1, keepdims=True)
        acc = acc + _dot(p.astype(BF16), v_ref[0, 0, pl.ds(kk, tq), :])
    o = acc / l
    o_ref[0] = jnp.where(lane < B_DH, o[:tq], o[tq:]).astype(BF16)


def _band_tables(rel_bias):
    t = ATT_BLOCK
    r = jnp.arange(t)[:, None]
    c = jnp.arange(t)[None, :]
    rc, cc = r // CHUNK, c // CHUNK
    tiles = []
    for back, allowed in ((0, cc <= rc), (1, jnp.ones((t, t), bool)), (2, cc >= rc)):
        idx = jnp.clip(back * t + r - c, -B_MAX_REL, B_MAX_REL) + B_MAX_REL
        tiles.append(jnp.where(allowed[None], rel_bias[:, idx].astype(F32), NEG))
    tiles.append(jnp.full((rel_bias.shape[0], t, t), NEG, F32))
    tab = jnp.stack(tiles, axis=1)
    tab = tab.reshape(B_HEADS // 2, 2, 4, t, t).transpose(0, 2, 1, 3, 4)
    return tab.reshape(B_HEADS // 2, 4, 2 * t, t)


def _band_attn(qkv, tab):
    b, _, s, w = qkv.shape
    tq = ATT_BLOCK
    npair = B_HEADS // 2
    return pl.pallas_call(
        _band_attn_kernel,
        grid=(npair, b, s // tq),
        in_specs=[pl.BlockSpec((1, 1, tq, w), lambda j, bi, i: (bi, 12 + j, i, 0)),
                  pl.BlockSpec((1, 1, s, w), lambda j, bi, i: (bi, 16 + j, 0, 0)),
                  pl.BlockSpec((1, 1, s, w), lambda j, bi, i: (bi, 20 + j, 0, 0)),
                  pl.BlockSpec((1, 4, 2 * tq, tq), lambda j, bi, i: (j, 0, 0, 0))],
        out_specs=pl.BlockSpec((1, tq, w), lambda j, bi, i: (bi, i, j)),
        out_shape=jax.ShapeDtypeStruct((b, s, B_HEADS * B_DH), BF16),
        compiler_params=_params(("arbitrary", "arbitrary", "arbitrary")),
        name="band_attn",
    )(qkv, qkv, qkv, tab)


def _post_mix_kernel(x_ref, ya_ref, yb_ref, gate_ref, km_ref, vm_ref,
                     wua_ref, wub_ref, wo_ref, gc_ref, wcq_ref, wco_ref,
                     gf_ref, wgu_ref, wd_ref, gfin_ref, o_ref, *, final_norm):
    d = x_ref.shape[2]
    ua = _dot(ya_ref[0], wua_ref[...])
    ub = _dot(yb_ref[0], wub_ref[...])
    merged = (gate_ref[0, :, :d].astype(F32) * ua + gate_ref[0, :, d:].astype(F32) * ub)
    x = x_ref[0] + _dot(merged.astype(BF16), wo_ref[...])

    dh = d // X_HEADS
    hc = _rms(x, gc_ref[...]).astype(BF16)
    q = (_dot(hc, wcq_ref[...]) * (dh ** -0.5)).astype(BF16)
    heads = []
    for h in range(X_HEADS):
        sl = slice(h * dh, (h + 1) * dh)
        s = _dot_nt(q[:, sl], km_ref[0, :, sl])
        p = jnp.exp(s - jnp.max(s, axis=1, keepdims=True))
        l = jnp.sum(p, axis=1, keepdims=True)
        heads.append((_dot(p.astype(BF16), vm_ref[0, :, sl]) / l).astype(BF16))
    x = x + _dot(jnp.concatenate(heads, axis=1), wco_ref[...])

    hf = _rms(x, gf_ref[...]).astype(BF16)
    dff = wd_ref.shape[0]
    acc = jnp.zeros_like(x)
    for c0 in range(0, dff, FF_CHUNK):
        c1 = min(c0 + FF_CHUNK, dff)
        g = _dot(hf, wgu_ref[:, c0:c1])
        u = _dot(hf, wgu_ref[:, dff + c0:dff + c1])
        a = (g * _sigmoid(g) * u).astype(BF16)
        acc = acc + _dot(a, wd_ref[c0:c1, :])
    x = x + acc
    if final_norm:
        x = _rms(x, gfin_ref[...])
    o_ref[0] = x


def _post_mix(x, ya, yb, gates, km, vm, wua, wub, wo, gc, wcq, wco, gf, wgu, wd, gfin,
              final_norm):
    b, s, d = x.shape
    ts = SEQ_TILE
    nm = km.shape[1]
    row = lambda w: pl.BlockSpec((1, ts, w), lambda bi, i: (bi, i, 0))
    mem = pl.BlockSpec((1, nm, d), lambda bi, i: (bi, 0, 0))
    kern = functools.partial(_post_mix_kernel, final_norm=final_norm)
    return pl.pallas_call(
        kern,
        grid=(b, s // ts),
        in_specs=[row(d), row(ya.shape[2]), row(yb.shape[2]), row(gates.shape[2]), mem, mem,
                  _const_spec(wua.shape), _const_spec(wub.shape), _const_spec(wo.shape),
                  _const_spec((1, d)), _const_spec(wcq.shape), _const_spec(wco.shape),
                  _const_spec((1, d)), _const_spec(wgu.shape), _const_spec(wd.shape),
                  _const_spec((1, d))],
        out_specs=row(d),
        out_shape=jax.ShapeDtypeStruct((b, s, d), F32),
        compiler_params=_params(("arbitrary", "arbitrary")),
        name="post_mix",
    )(x, ya, yb, gates, km, vm, wua, wub, wo, gc, wcq, wco, gf, wgu, wd, gfin)


def _rope_tables(seq):
    half = A_DK // 2
    inv = 1.0 / (ROPE_THETA ** (jnp.arange(0, A_DK, 2, dtype=F32) / A_DK))
    ang = jnp.arange(seq, dtype=F32)[:, None] * inv[None, :]
    cos, sin = jnp.cos(ang), jnp.sin(ang)
    reps = LANES_V7X // A_DK
    cos_t = jnp.tile(jnp.concatenate([cos, cos], axis=1), (1, reps))
    sin_t = jnp.tile(jnp.concatenate([-sin, sin], axis=1), (1, reps))
    return cos_t, sin_t


def kernel(x, mem, norm_mix_g, w_in, lam_q1, lam_k1, lam_q2, lam_k2, subln_g, rel_bias, w_up_a, w_up_b, w_out, norm_cross_g, norm_mem_g, w_cq, w_ckv, w_co, norm_ffn_g, w_gate_up, w_down, norm_final_g):
    b, s, d = x.shape
    depth = w_in.shape[0]
    assert s % SEQ_TILE == 0 and s % ATT_BLOCK == 0 and ATT_BLOCK == 4 * CHUNK
    assert B_LEFT_CHUNKS * CHUNK == 2 * ATT_BLOCK and B_MAX_REL <= ATT_BLOCK
    cos_t, sin_t = _rope_tables(s)
    row = lambda v: v.reshape(1, -1).astype(F32)
    for layer in range(depth):
        lam_init = 0.8 - 0.6 * math.exp(-0.3 * layer)
        km, vm = _mem_kv(mem, row(norm_mem_g[layer]), w_ckv[layer].astype(BF16))
        qkv, gates = _in_proj(x, row(norm_mix_g[layer]), w_in[layer].astype(BF16), cos_t, sin_t)
        ya = _diff_attn(qkv, row(lam_q1[layer]), row(lam_k1[layer]), row(lam_q2[layer]),
                        row(lam_k2[layer]), row(subln_g[layer]), lam_init)
        yb = _band_attn(qkv, _band_tables(rel_bias[layer]))
        x = _post_mix(x, ya, yb, gates, km, vm,
                      w_up_a[layer].astype(BF16), w_up_b[layer].astype(BF16),
                      w_out[layer].astype(BF16), row(norm_cross_g[layer]),
                      w_cq[layer].astype(BF16), w_co[layer].astype(BF16),
                      row(norm_ffn_g[layer]), w_gate_up[layer].astype(BF16),
                      w_down[layer].astype(BF16), row(norm_final_g),
                      final_norm=(layer == depth - 1))
    return x
```

```python
import functools
import math

import jax
import jax.numpy as jnp
from jax import lax
from jax.experimental import pallas as pl
from jax.experimental.pallas import tpu as pltpu

CHUNK = 64
EPS = 1e-6
ROPE_THETA = 10000.0
NEG = -1e30

A_HEADS = 4
A_DK = 64
A_DV = 2 * A_DK
B_HEADS = 8
B_DH = 64
B_LEFT_CHUNKS = 8
B_MAX_REL = 256
X_HEADS = 4

LANES_V7X = 128
VMEM_LIMIT_BYTES_V7X = 56 * 1024 * 1024

SEQ_TILE = 512
ATT_BLOCK = 256
FF_CHUNK = 512

BF16 = jnp.bfloat16
F32 = jnp.float32


def _rms(x, g):
    return x * lax.rsqrt(jnp.mean(x * x, axis=-1, keepdims=True) + EPS) * g


def _dot(a, b):
    return jnp.dot(a, b, preferred_element_type=F32)


def _dot_nt(a, b):
    return lax.dot_general(a, b, (((1,), (1,)), ((), ())), preferred_element_type=F32)


def _sigmoid(x):
    return 1.0 / (1.0 + jnp.exp(-x))


def _const_spec(shape):
    n = len(shape)
    return pl.BlockSpec(shape, lambda *_: (0,) * n, pipeline_mode=pl.Buffered(1))


def _params(semantics):
    return pltpu.CompilerParams(dimension_semantics=semantics,
                                vmem_limit_bytes=VMEM_LIMIT_BYTES_V7X)


def _mem_kv_kernel(mem_ref, g_ref, w_ref, k_ref, v_ref):
    nb, nm, d = mem_ref.shape
    m = mem_ref[...].reshape(nb * nm, d)
    mn = _rms(m, g_ref[...]).astype(BF16)
    kv = _dot(mn, w_ref[...])
    k_ref[...] = kv[:, :d].astype(BF16).reshape(nb, nm, d)
    v_ref[...] = kv[:, d:].astype(BF16).reshape(nb, nm, d)


def _mem_kv(mem, g, w_ckv):
    b, nm, d = mem.shape
    nb = 2 if b % 2 == 0 else 1
    return pl.pallas_call(
        _mem_kv_kernel,
        grid=(b // nb,),
        in_specs=[pl.BlockSpec((nb, nm, d), lambda i: (i, 0, 0)),
                  _const_spec((1, d)),
                  _const_spec((d, 2 * d))],
        out_specs=[pl.BlockSpec((nb, nm, d), lambda i: (i, 0, 0))] * 2,
        out_shape=[jax.ShapeDtypeStruct((b, nm, d), BF16)] * 2,
        compiler_params=_params(("arbitrary",)),
        name="mem_kv",
    )(mem, g, w_ckv)


def _in_proj_kernel(x_ref, g_ref, w_ref, cos_ref, sin_ref, qkv_ref, gate_ref, *, q_scale):
    ts = x_ref.shape[1]
    h = _rms(x_ref[0], g_ref[...]).astype(BF16)
    cos = cos_ref[...]
    sin = sin_ref[...]
    lane = lax.broadcasted_iota(jnp.int32, (ts, LANES_V7X), 1)
    first_half = (lane % A_DK) < (A_DK // 2)
    slab_w = LANES_V7X
    grp_w = 4 * slab_w
    for grp in range(6):
        z = _dot(h, w_ref[:, grp * grp_w:(grp + 1) * grp_w])
        for s in range(4):
            zs = z[:, s * slab_w:(s + 1) * slab_w]
            if grp < 2:
                partner = jnp.where(first_half,
                                    pltpu.roll(zs, LANES_V7X - A_DK // 2, 1),
                                    pltpu.roll(zs, A_DK // 2, 1))
                zs = zs * cos + partner * sin
            if grp in (0, 3):
                zs = zs * q_scale
            qkv_ref[0, grp * 4 + s] = zs.astype(BF16)
    gate0 = 6 * grp_w
    for c in range(gate_ref.shape[2] // grp_w):
        z = _dot(h, w_ref[:, gate0 + c * grp_w: gate0 + (c + 1) * grp_w])
        gate_ref[0, :, c * grp_w:(c + 1) * grp_w] = _sigmoid(z).astype(BF16)


def _in_proj(x, g, w_in, cos_t, sin_t):
    b, s, d = x.shape
    ts = SEQ_TILE
    n_gate = w_in.shape[1] - 6 * 4 * LANES_V7X
    kern = functools.partial(_in_proj_kernel, q_scale=A_DK ** -0.5)
    return pl.pallas_call(
        kern,
        grid=(b, s // ts),
        in_specs=[pl.BlockSpec((1, ts, d), lambda bi, i: (bi, i, 0)),
                  _const_spec((1, d)),
                  _const_spec(w_in.shape),
                  pl.BlockSpec((ts, LANES_V7X), lambda bi, i: (i, 0)),
                  pl.BlockSpec((ts, LANES_V7X), lambda bi, i: (i, 0))],
        out_specs=[pl.BlockSpec((1, 24, ts, LANES_V7X), lambda bi, i: (bi, 0, i, 0)),
                   pl.BlockSpec((1, ts, n_gate), lambda bi, i: (bi, i, 0))],
        out_shape=[jax.ShapeDtypeStruct((b, 24, s, LANES_V7X), BF16),
                   jax.ShapeDtypeStruct((b, s, n_gate), BF16)],
        compiler_params=_params(("arbitrary", "arbitrary")),
        name="in_proj",
    )(x, g, w_in, cos_t, sin_t)


def _stack_halves(q, split):
    lane = lax.broadcasted_iota(jnp.int32, q.shape, 1)
    zero = jnp.zeros_like(q)
    return jnp.concatenate([jnp.where(lane < split, q, zero),
                            jnp.where(lane >= split, q, zero)], axis=0)


def _row_softmax_pv(s, v):
    p = jnp.exp(s - jnp.max(s, axis=1, keepdims=True))
    return _dot(p.astype(BF16), v), jnp.sum(p, axis=1, keepdims=True)


def _diff_attn_kernel(q_ref, k_ref, v_ref, lq1_ref, lk1_ref, lq2_ref, lk2_ref, g_ref,
                      o_ref, *, lam_init):
    tq = ATT_BLOCK
    seq = q_ref.shape[2]
    lam = (jnp.exp(jnp.sum(lq1_ref[...] * lk1_ref[...], axis=1, keepdims=True))
           - jnp.exp(jnp.sum(lq2_ref[...] * lk2_ref[...], axis=1, keepdims=True))
           + lam_init)
    row = lax.broadcasted_iota(jnp.int32, (2 * tq, tq), 0)
    col = lax.broadcasted_iota(jnp.int32, (2 * tq, tq), 1)
    allowed = (col // CHUNK) <= ((row % tq) // CHUNK)
    for qi in range(seq // tq):
        q0, q1 = qi * tq, (qi + 1) * tq
        qq = _stack_halves(q_ref[0, 0, q0:q1, :], A_DK)
        s_own = jnp.where(allowed, _dot_nt(qq, k_ref[0, 0, q0:q1, :]), NEG)
        if qi:
            s = jnp.concatenate([_dot_nt(qq, k_ref[0, 0, :q0, :]), s_own], axis=1)
        else:
            s = s_own
        pv, l = _row_softmax_pv(s, v_ref[0, 0, :q1, :])
        o = pv / l
        o = o[:tq] - lam * o[tq:]
        o = _rms(o, g_ref[...]) * (1.0 - lam_init)
        o_ref[0, q0:q1, :] = o.astype(BF16)


def _diff_attn(qkv, lq1, lk1, lq2, lk2, subln_g, lam_init):
    b, _, s, w = qkv.shape
    kern = functools.partial(_diff_attn_kernel, lam_init=lam_init)
    vec = _const_spec((1, A_DK))
    slab = lambda g0: pl.BlockSpec((1, 1, s, w), lambda bi, h: (bi, g0 + h, 0, 0))
    return pl.pallas_call(
        kern,
        grid=(b, A_HEADS),
        in_specs=[slab(0), slab(4), slab(8), vec, vec, vec, vec, _const_spec((1, A_DV))],
        out_specs=pl.BlockSpec((1, s, w), lambda bi, h: (bi, 0, h)),
        out_shape=jax.ShapeDtypeStruct((b, s, A_HEADS * A_DV), BF16),
        compiler_params=_params(("arbitrary", "arbitrary")),
        name="diff_attn",
    )(qkv, qkv, qkv, lq1, lk1, lq2, lk2, subln_g)


def _band_tiles_init(bias_ref, tab_ref):
    t = ATT_BLOCK
    rc = lax.broadcasted_iota(jnp.int32, (t, t), 0) // CHUNK
    cc = lax.broadcasted_iota(jnp.int32, (t, t), 1) // CHUNK
    for head in range(2):
        rows = slice(head * t, (head + 1) * t)
        for back in range(2):
            line = jnp.broadcast_to(bias_ref[0, head, back:back + 1, :], (t, 2 * t))
            tile = pltpu.roll(line, 0, 1, stride=1, stride_axis=0)[:, :t]
            tab_ref[back, rows, :] = jnp.where(cc <= rc, tile, NEG) if back == 0 else tile
        far = jnp.broadcast_to(bias_ref[0, head, 1:2, 0:1], (t, t))
        tab_ref[2, rows, :] = jnp.where(cc >= rc, far, NEG)


def _band_attn_kernel(q_ref, k_ref, v_ref, bias_ref, o_ref, tab_ref):
    tq = ATT_BLOCK
    seq = q_ref.shape[2]

    @pl.when(pl.program_id(1) == 0)
    def _():
        _band_tiles_init(bias_ref, tab_ref)

    lane = lax.broadcasted_iota(jnp.int32, (tq, LANES_V7X), 1)
    for qi in range(seq // tq):
        q0, q1 = qi * tq, (qi + 1) * tq
        nblk = min(qi, 2) + 1
        k0 = q1 - nblk * tq
        qq = _stack_halves(q_ref[0, 0, q0:q1, :], B_DH)
        bias = jnp.concatenate([tab_ref[back] for back in reversed(range(nblk))], axis=1)
        s = _dot_nt(qq, k_ref[0, 0, k0:q1, :]) + bias
        pv, l = _row_softmax_pv(s, v_ref[0, 0, k0:q1, :])
        o = pv / l
        o_ref[0, q0:q1, :] = jnp.where(lane < B_DH, o[:tq], o[tq:]).astype(BF16)


def _band_bias_lines(rel_bias):
    t = ATT_BLOCK
    u = jnp.arange(2 * t)
    u = jnp.where(u < t, u, u - 2 * t)
    idx = jnp.stack([jnp.clip(-u, -B_MAX_REL, B_MAX_REL), jnp.clip(t - u, -B_MAX_REL, B_MAX_REL)])
    lines = rel_bias[:, idx + B_MAX_REL].astype(F32)
    return lines.reshape(B_HEADS // 2, 2, 2, 2 * t)


def _band_attn(qkv, lines):
    b, _, s, w = qkv.shape
    tq = ATT_BLOCK
    npair = B_HEADS // 2
    slab = lambda g0: pl.BlockSpec((1, 1, s, w), lambda j, bi: (bi, g0 + j, 0, 0))
    return pl.pallas_call(
        _band_attn_kernel,
        grid=(npair, b),
        in_specs=[slab(12), slab(16), slab(20),
                  pl.BlockSpec((1, 2, 2, 2 * tq), lambda j, bi: (j, 0, 0, 0))],
        out_specs=pl.BlockSpec((1, s, w), lambda j, bi: (bi, 0, j)),
        out_shape=jax.ShapeDtypeStruct((b, s, B_HEADS * B_DH), BF16),
        scratch_shapes=[pltpu.VMEM((3, 2 * tq, tq), F32)],
        compiler_params=_params(("arbitrary", "arbitrary")),
        name="band_attn",
    )(qkv, qkv, qkv, lines)


def _post_mix_kernel(x_ref, ya_ref, yb_ref, gate_ref, km_ref, vm_ref,
                     wua_ref, wub_ref, wo_ref, gc_ref, wcq_ref, wco_ref,
                     gf_ref, wgu_ref, wd_ref, gfin_ref, o_ref, *, final_norm):
    d = x_ref.shape[2]
    ua = _dot(ya_ref[0], wua_ref[...])
    ub = _dot(yb_ref[0], wub_ref[...])
    merged = (gate_ref[0, :, :d].astype(F32) * ua + gate_ref[0, :, d:].astype(F32) * ub)
    x = x_ref[0] + _dot(merged.astype(BF16), wo_ref[...])

    dh = d // X_HEADS
    hc = _rms(x, gc_ref[...]).astype(BF16)
    q = (_dot(hc, wcq_ref[...]) * (dh ** -0.5)).astype(BF16)
    heads = []
    for h in range(X_HEADS):
        sl = slice(h * dh, (h + 1) * dh)
        pv, l = _row_softmax_pv(_dot_nt(q[:, sl], km_ref[0, :, sl]), vm_ref[0, :, sl])
        heads.append((pv / l).astype(BF16))
    x = x + _dot(jnp.concatenate(heads, axis=1), wco_ref[...])

    hf = _rms(x, gf_ref[...]).astype(BF16)
    dff = wd_ref.shape[0]
    acc = jnp.zeros_like(x)
    for c0 in range(0, dff, FF_CHUNK):
        c1 = min(c0 + FF_CHUNK, dff)
        g = _dot(hf, wgu_ref[:, c0:c1])
        u = _dot(hf, wgu_ref[:, dff + c0:dff + c1])
        a = (g * _sigmoid(g) * u).astype(BF16)
        acc = acc + _dot(a, wd_ref[c0:c1, :])
    x = x + acc
    if final_norm:
        x = _rms(x, gfin_ref[...])
    o_ref[0] = x


def _post_mix(x, ya, yb, gates, km, vm, wua, wub, wo, gc, wcq, wco, gf, wgu, wd, gfin,
              final_norm):
    b, s, d = x.shape
    ts = SEQ_TILE
    nm = km.shape[1]
    row = lambda w: pl.BlockSpec((1, ts, w), lambda bi, i: (bi, i, 0))
    mem = pl.BlockSpec((1, nm, d), lambda bi, i: (bi, 0, 0))
    kern = functools.partial(_post_mix_kernel, final_norm=final_norm)
    return pl.pallas_call(
        kern,
        grid=(b, s // ts),
        in_specs=[row(d), row(ya.shape[2]), row(yb.shape[2]), row(gates.shape[2]), mem, mem,
                  _const_spec(wua.shape), _const_spec(wub.shape), _const_spec(wo.shape),
                  _const_spec((1, d)), _const_spec(wcq.shape), _const_spec(wco.shape),
                  _const_spec((1, d)), _const_spec(wgu.shape), _const_spec(wd.shape),
                  _const_spec((1, d))],
        out_specs=row(d),
        out_shape=jax.ShapeDtypeStruct((b, s, d), F32),
        compiler_params=_params(("arbitrary", "arbitrary")),
        name="post_mix",
    )(x, ya, yb, gates, km, vm, wua, wub, wo, gc, wcq, wco, gf, wgu, wd, gfin)


def _rope_tables(seq):
    inv = 1.0 / (ROPE_THETA ** (jnp.arange(0, A_DK, 2, dtype=F32) / A_DK))
    ang = jnp.arange(seq, dtype=F32)[:, None] * inv[None, :]
    cos, sin = jnp.cos(ang), jnp.sin(ang)
    reps = LANES_V7X // A_DK
    cos_t = jnp.tile(jnp.concatenate([cos, cos], axis=1), (1, reps))
    sin_t = jnp.tile(jnp.concatenate([-sin, sin], axis=1), (1, reps))
    return cos_t, sin_t


def kernel(x, mem, norm_mix_g, w_in, lam_q1, lam_k1, lam_q2, lam_k2, subln_g, rel_bias, w_up_a, w_up_b, w_out, norm_cross_g, norm_mem_g, w_cq, w_ckv, w_co, norm_ffn_g, w_gate_up, w_down, norm_final_g):
    b, s, d = x.shape
    depth = w_in.shape[0]
    assert s % SEQ_TILE == 0 and s % ATT_BLOCK == 0 and ATT_BLOCK == 4 * CHUNK
    assert B_LEFT_CHUNKS * CHUNK == 2 * ATT_BLOCK and B_MAX_REL <= ATT_BLOCK
    cos_t, sin_t = _rope_tables(s)
    row = lambda v: v.reshape(1, -1).astype(F32)
    for layer in range(depth):
        lam_init = 0.8 - 0.6 * math.exp(-0.3 * layer)
        km, vm = _mem_kv(mem, row(norm_mem_g[layer]), w_ckv[layer].astype(BF16))
        qkv, gates = _in_proj(x, row(norm_mix_g[layer]), w_in[layer].astype(BF16), cos_t, sin_t)
        ya = _diff_attn(qkv, row(lam_q1[layer]), row(lam_k1[layer]), row(lam_q2[layer]),
                        row(lam_k2[layer]), row(subln_g[layer]), lam_init)
        yb = _band_attn(qkv, _band_bias_lines(rel_bias[layer]))
        x = _post_mix(x, ya, yb, gates, km, vm,
                      w_up_a[layer].astype(BF16), w_up_b[layer].astype(BF16),
                      w_out[layer].astype(BF16), row(norm_cross_g[layer]),
                      w_cq[layer].astype(BF16), w_co[layer].astype(BF16),
                      row(norm_ffn_g[layer]), w_gate_up[layer].astype(BF16),
                      w_down[layer].astype(BF16), row(norm_final_g),
                      final_norm=(layer == depth - 1))
    return x
```

```python
import functools
import math

import jax
import jax.numpy as jnp
from jax import lax
from jax.experimental import pallas as pl
from jax.experimental.pallas import tpu as pltpu

CHUNK = 64
EPS = 1e-6
ROPE_THETA = 10000.0
NEG = -1e30

A_HEADS = 4
A_DK = 64
A_DV = 2 * A_DK
B_HEADS = 8
B_DH = 64
B_LEFT_CHUNKS = 8
B_MAX_REL = 256
X_HEADS = 4

LANES_V7X = 128
VMEM_LIMIT_BYTES_V7X = 56 * 1024 * 1024

IN_TILE = 1024
SEQ_TILE = 512
ATT_BLOCK = 256
FF_CHUNK = 512

BF16 = jnp.bfloat16
F32 = jnp.float32
LOG2E = math.log2(math.e)


def _rms(x, g):
    return x * lax.rsqrt(jnp.mean(x * x, axis=-1, keepdims=True) + EPS) * g


def _dot(a, b):
    return jnp.dot(a, b, preferred_element_type=F32)


def _dot_nt(a, b):
    return lax.dot_general(a, b, (((1,), (1,)), ((), ())), preferred_element_type=F32)


def _sigmoid(x):
    return 1.0 / (1.0 + jnp.exp(-x))


def _const_spec(shape):
    n = len(shape)
    return pl.BlockSpec(shape, lambda *_: (0,) * n, pipeline_mode=pl.Buffered(1))


def _params(semantics, flags=None):
    return pltpu.CompilerParams(dimension_semantics=semantics,
                                vmem_limit_bytes=VMEM_LIMIT_BYTES_V7X, flags=flags)


def _mem_kv_kernel(mem_ref, g_ref, w_ref, k_ref, v_ref):
    nb, nm, d = mem_ref.shape
    m = mem_ref[...].reshape(nb * nm, d)
    mn = _rms(m, g_ref[...]).astype(BF16)
    kv = _dot(mn, w_ref[...])
    k_ref[...] = kv[:, :d].astype(BF16).reshape(nb, nm, d)
    v_ref[...] = kv[:, d:].astype(BF16).reshape(nb, nm, d)


def _mem_kv(mem, g, w_ckv):
    b, nm, d = mem.shape
    nb = 2 if b % 2 == 0 else 1
    return pl.pallas_call(
        _mem_kv_kernel,
        grid=(b // nb,),
        in_specs=[pl.BlockSpec((nb, nm, d), lambda i: (i, 0, 0)),
                  _const_spec((1, d)),
                  _const_spec((d, 2 * d))],
        out_specs=[pl.BlockSpec((nb, nm, d), lambda i: (i, 0, 0))] * 2,
        out_shape=[jax.ShapeDtypeStruct((b, nm, d), BF16)] * 2,
        compiler_params=_params(("arbitrary",)),
        name="mem_kv",
    )(mem, g, w_ckv)


def _in_proj_kernel(x_ref, g_ref, w_ref, cos_ref, sin_ref, qkv_ref, gate_ref, *, q_scale):
    ts = x_ref.shape[1]
    h = _rms(x_ref[0], g_ref[...]).astype(BF16)
    cos = cos_ref[...]
    sin = sin_ref[...]
    lane = lax.broadcasted_iota(jnp.int32, (ts, LANES_V7X), 1)
    first_half = (lane % A_DK) < (A_DK // 2)
    slab_w = LANES_V7X
    grp_w = 4 * slab_w
    for grp in range(6):
        z = _dot(h, w_ref[:, grp * grp_w:(grp + 1) * grp_w])
        for s in range(4):
            zs = z[:, s * slab_w:(s + 1) * slab_w]
            if grp < 2:
                partner = jnp.where(first_half,
                                    pltpu.roll(zs, LANES_V7X - A_DK // 2, 1),
                                    pltpu.roll(zs, A_DK // 2, 1))
                zs = zs * cos + partner * sin
            if grp in (0, 3):
                zs = zs * q_scale
            qkv_ref[0, grp * 4 + s] = zs.astype(BF16)
    gate0 = 6 * grp_w
    for c in range(gate_ref.shape[2] // grp_w):
        z = _dot(h, w_ref[:, gate0 + c * grp_w: gate0 + (c + 1) * grp_w])
        gate_ref[0, :, c * grp_w:(c + 1) * grp_w] = _sigmoid(z).astype(BF16)


def _in_proj(x, g, w_in, cos_t, sin_t):
    b, s, d = x.shape
    ts = IN_TILE
    n_gate = w_in.shape[1] - 6 * 4 * LANES_V7X
    assert A_DK == B_DH
    kern = functools.partial(_in_proj_kernel, q_scale=A_DK ** -0.5 * LOG2E)
    return pl.pallas_call(
        kern,
        grid=(b, s // ts),
        in_specs=[pl.BlockSpec((1, ts, d), lambda bi, i: (bi, i, 0)),
                  _const_spec((1, d)),
                  _const_spec(w_in.shape),
                  pl.BlockSpec((ts, LANES_V7X), lambda bi, i: (i, 0)),
                  pl.BlockSpec((ts, LANES_V7X), lambda bi, i: (i, 0))],
        out_specs=[pl.BlockSpec((1, 24, ts, LANES_V7X), lambda bi, i: (bi, 0, i, 0)),
                   pl.BlockSpec((1, ts, n_gate), lambda bi, i: (bi, i, 0))],
        out_shape=[jax.ShapeDtypeStruct((b, 24, s, LANES_V7X), BF16),
                   jax.ShapeDtypeStruct((b, s, n_gate), BF16)],
        compiler_params=_params(("arbitrary", "arbitrary")),
        name="in_proj",
    )(x, g, w_in, cos_t, sin_t)


def _stack_halves(q, split):
    lane = lax.broadcasted_iota(jnp.int32, q.shape, 1)
    zero = jnp.zeros_like(q)
    return jnp.concatenate([jnp.where(lane < split, q, zero),
                            jnp.where(lane >= split, q, zero)], axis=0)


def _row_softmax_pv(s, v):
    p = jnp.exp(s - jnp.max(s, axis=1, keepdims=True))
    return _dot(p.astype(BF16), v), jnp.sum(p, axis=1, keepdims=True)


def _diff_attn_kernel(q_ref, k_ref, v_ref, lq1_ref, lk1_ref, lq2_ref, lk2_ref, g_ref,
                      o_ref, vt_ref, s0_ref, s1_ref, p0_ref, p1_ref, *, lam_init):
    tq = ATT_BLOCK
    seq = q_ref.shape[2]
    vt_ref[...] = v_ref[0, 0].T
    lam = (jnp.exp(jnp.sum(lq1_ref[...] * lk1_ref[...], axis=1, keepdims=True))
           - jnp.exp(jnp.sum(lq2_ref[...] * lk2_ref[...], axis=1, keepdims=True))
           + lam_init)
    key = lax.broadcasted_iota(jnp.int32, (tq, 2 * tq), 0)
    qry = lax.broadcasted_iota(jnp.int32, (tq, 2 * tq), 1)
    allowed = (key // CHUNK) <= ((qry % tq) // CHUNK)

    s_refs, p_refs = (s0_ref, s1_ref), (p0_ref, p1_ref)

    def scores(qi):
        qq = _stack_halves(q_ref[0, 0, qi * tq:(qi + 1) * tq, :], A_DK)
        m8 = None
        for kj in range(qi + 1):
            rows = slice(kj * tq, (kj + 1) * tq)
            s = _dot_nt(k_ref[0, 0, rows, :], qq)
            if kj == qi:
                s = jnp.where(allowed, s, NEG)
            s_refs[qi % 2][rows, :] = s
            mj = jnp.max(s.reshape(tq // 8, 8, 2 * tq), axis=0)
            m8 = mj if m8 is None else jnp.maximum(m8, mj)
        return jnp.max(m8, axis=0, keepdims=True)

    def probs(qi, m):
        l8 = None
        for kj in range(qi + 1):
            rows = slice(kj * tq, (kj + 1) * tq)
            p = jnp.exp2(s_refs[qi % 2][rows, :] - m)
            p_refs[qi % 2][rows, :] = p.astype(BF16)
            lj = jnp.sum(p.reshape(tq // 8, 8, 2 * tq), axis=0)
            l8 = lj if l8 is None else l8 + lj
        return jnp.sum(l8, axis=0, keepdims=True)

    def values(qi, l):
        q0, q1 = qi * tq, (qi + 1) * tq
        ot = _dot(vt_ref[:, :q1], p_refs[qi % 2][:q1, :]) / l
        o = (ot[:, :tq] - lam * ot[:, tq:]).T
        o = _rms(o, g_ref[...]) * (1.0 - lam_init)
        o_ref[0, q0:q1, :] = o.astype(BF16)

    nq = seq // tq
    m_cur = scores(0)
    pending = None
    for qi in range(nq):
        m_next = scores(qi + 1) if qi + 1 < nq else None
        if pending is not None:
            values(*pending)
        pending = (qi, probs(qi, m_cur))
        m_cur = m_next
    values(*pending)


def _diff_attn(qkv, lq1, lk1, lq2, lk2, subln_g, lam_init):
    b, _, s, w = qkv.shape
    kern = functools.partial(_diff_attn_kernel, lam_init=lam_init)
    vec = _const_spec((1, A_DK))
    slab = lambda g0: pl.BlockSpec((1, 1, s, w), lambda bi, h: (bi, g0 + h, 0, 0))
    return pl.pallas_call(
        kern,
        grid=(b, A_HEADS),
        in_specs=[slab(0), slab(4), slab(8), vec, vec, vec, vec, _const_spec((1, A_DV))],
        out_specs=pl.BlockSpec((1, s, w), lambda bi, h: (bi, 0, h)),
        out_shape=jax.ShapeDtypeStruct((b, s, A_HEADS * A_DV), BF16),
        scratch_shapes=[pltpu.VMEM((w, s), BF16),
                        pltpu.VMEM((s, 2 * ATT_BLOCK), F32), pltpu.VMEM((s, 2 * ATT_BLOCK), F32),
                        pltpu.VMEM((s, 2 * ATT_BLOCK), BF16), pltpu.VMEM((s, 2 * ATT_BLOCK), BF16)],
        compiler_params=_params(("arbitrary", "arbitrary")),
        name="diff_attn",
    )(qkv, qkv, qkv, lq1, lk1, lq2, lk2, subln_g)


def _band_tiles_init(bias_ref, tab_ref):
    t = ATT_BLOCK
    kc = lax.broadcasted_iota(jnp.int32, (t, t), 0) // CHUNK
    qc = lax.broadcasted_iota(jnp.int32, (t, t), 1) // CHUNK
    for head in range(2):
        cols = slice(head * t, (head + 1) * t)
        for back in range(2):
            line = jnp.broadcast_to(bias_ref[0, head, back:back + 1, :], (t, 2 * t))
            tile = pltpu.roll(line, 0, 1, stride=1, stride_axis=0)[:, :t]
            tab_ref[back, :, cols] = jnp.where(kc <= qc, tile, NEG) if back == 0 else tile
        far = jnp.broadcast_to(bias_ref[0, head, 1:2, 0:1], (t, t))
        tab_ref[2, :, cols] = jnp.where(kc >= qc, far, NEG)


def _band_attn_kernel(q_ref, k_ref, v_ref, bias_ref, o_ref, tab_ref, vt_ref,
                      s0_ref, s1_ref, p0_ref, p1_ref):
    tq = ATT_BLOCK
    seq = q_ref.shape[2]

    @pl.when(pl.program_id(1) == 0)
    def _():
        _band_tiles_init(bias_ref, tab_ref)

    vt_ref[...] = v_ref[0, 0].T
    first_head = lax.broadcasted_iota(jnp.int32, (LANES_V7X, tq), 0) < B_DH
    s_refs, p_refs = (s0_ref, s1_ref), (p0_ref, p1_ref)

    def scores(qi):
        qq = _stack_halves(q_ref[0, 0, qi * tq:(qi + 1) * tq, :], B_DH)
        m8 = None
        for back in range(min(qi, 2) + 1):
            kj = qi - back
            s = _dot_nt(k_ref[0, 0, kj * tq:(kj + 1) * tq, :], qq) + tab_ref[back]
            s_refs[qi % 2][back * tq:(back + 1) * tq, :] = s
            mj = jnp.max(s.reshape(tq // 8, 8, 2 * tq), axis=0)
            m8 = mj if m8 is None else jnp.maximum(m8, mj)
        return jnp.max(m8, axis=0, keepdims=True)

    def probs(qi, m):
        nblk = min(qi, 2) + 1
        l8 = None
        for back in range(nblk):
            p = jnp.exp2(s_refs[qi % 2][back * tq:(back + 1) * tq, :] - m)
            pos = nblk - 1 - back
            p_refs[qi % 2][pos * tq:(pos + 1) * tq, :] = p.astype(BF16)
            lj = jnp.sum(p.reshape(tq // 8, 8, 2 * tq), axis=0)
            l8 = lj if l8 is None else l8 + lj
        return jnp.sum(l8, axis=0, keepdims=True)

    def values(qi, l):
        q0, q1 = qi * tq, (qi + 1) * tq
        nblk = min(qi, 2) + 1
        ot = _dot(vt_ref[:, q1 - nblk * tq:q1], p_refs[qi % 2][:nblk * tq, :]) / l
        o = jnp.where(first_head, ot[:, :tq], ot[:, tq:]).T
        o_ref[0, q0:q1, :] = o.astype(BF16)

    nq = seq // tq
    m_cur = scores(0)
    pending = None
    for qi in range(nq):
        m_next = scores(qi + 1) if qi + 1 < nq else None
        if pending is not None:
            values(*pending)
        pending = (qi, probs(qi, m_cur))
        m_cur = m_next
    values(*pending)


def _band_bias_lines(rel_bias):
    t = ATT_BLOCK
    u = jnp.arange(2 * t)
    u = jnp.where(u < t, u, u - 2 * t)
    idx = jnp.stack([jnp.clip(u, -B_MAX_REL, B_MAX_REL), jnp.clip(t + u, -B_MAX_REL, B_MAX_REL)])
    lines = rel_bias[:, idx + B_MAX_REL].astype(F32) * LOG2E
    return lines.reshape(B_HEADS // 2, 2, 2, 2 * t)


def _band_attn(qkv, lines):
    b, _, s, w = qkv.shape
    tq = ATT_BLOCK
    npair = B_HEADS // 2
    slab = lambda g0: pl.BlockSpec((1, 1, s, w), lambda j, bi: (bi, g0 + j, 0, 0))
    return pl.pallas_call(
        _band_attn_kernel,
        grid=(npair, b),
        in_specs=[slab(12), slab(16), slab(20),
                  pl.BlockSpec((1, 2, 2, 2 * tq), lambda j, bi: (j, 0, 0, 0))],
        out_specs=pl.BlockSpec((1, s, w), lambda j, bi: (bi, 0, j)),
        out_shape=jax.ShapeDtypeStruct((b, s, B_HEADS * B_DH), BF16),
        scratch_shapes=[pltpu.VMEM((3, tq, 2 * tq), F32), pltpu.VMEM((w, s), BF16),
                        pltpu.VMEM((3 * tq, 2 * tq), F32), pltpu.VMEM((3 * tq, 2 * tq), F32),
                        pltpu.VMEM((3 * tq, 2 * tq), BF16), pltpu.VMEM((3 * tq, 2 * tq), BF16)],
        compiler_params=_params(("arbitrary", "arbitrary")),
        name="band_attn",
    )(qkv, qkv, qkv, lines)


def _post_mix_kernel(x_ref, ya_ref, yb_ref, gate_ref, km_ref, vm_ref,
                     wua_ref, wub_ref, wo_ref, gc_ref, wcq_ref, wco_ref,
                     gf_ref, wgu_ref, wd_ref, gfin_ref, o_ref, *, final_norm):
    d = x_ref.shape[2]
    ua = _dot(ya_ref[0], wua_ref[...])
    ub = _dot(yb_ref[0], wub_ref[...])
    merged = (gate_ref[0, :, :d].astype(F32) * ua + gate_ref[0, :, d:].astype(F32) * ub)
    x = x_ref[0] + _dot(merged.astype(BF16), wo_ref[...])

    dh = d // X_HEADS
    hc = _rms(x, gc_ref[...]).astype(BF16)
    q = (_dot(hc, wcq_ref[...]) * (dh ** -0.5)).astype(BF16)
    heads = []
    for h in range(X_HEADS):
        sl = slice(h * dh, (h + 1) * dh)
        pv, l = _row_softmax_pv(_dot_nt(q[:, sl], km_ref[0, :, sl]), vm_ref[0, :, sl])
        heads.append((pv / l).astype(BF16))
    x = x + _dot(jnp.concatenate(heads, axis=1), wco_ref[...])

    hf = _rms(x, gf_ref[...]).astype(BF16)
    dff = wd_ref.shape[0]
    acc = jnp.zeros_like(x)
    for c0 in range(0, dff, FF_CHUNK):
        c1 = min(c0 + FF_CHUNK, dff)
        g = _dot(hf, wgu_ref[:, c0:c1])
        u = _dot(hf, wgu_ref[:, dff + c0:dff + c1])
        a = (g * _sigmoid(g) * u).astype(BF16)
        acc = acc + _dot(a, wd_ref[c0:c1, :])
    x = x + acc
    if final_norm:
        x = _rms(x, gfin_ref[...])
    o_ref[0] = x


def _post_mix(x, ya, yb, gates, km, vm, wua, wub, wo, gc, wcq, wco, gf, wgu, wd, gfin,
              final_norm):
    b, s, d = x.shape
    ts = SEQ_TILE
    nm = km.shape[1]
    row = lambda w: pl.BlockSpec((1, ts, w), lambda bi, i: (bi, i, 0))
    mem = pl.BlockSpec((1, nm, d), lambda bi, i: (bi, 0, 0))
    kern = functools.partial(_post_mix_kernel, final_norm=final_norm)
    return pl.pallas_call(
        kern,
        grid=(b, s // ts),
        in_specs=[row(d), row(ya.shape[2]), row(yb.shape[2]), row(gates.shape[2]), mem, mem,
                  _const_spec(wua.shape), _const_spec(wub.shape), _const_spec(wo.shape),
                  _const_spec((1, d)), _const_spec(wcq.shape), _const_spec(wco.shape),
                  _const_spec((1, d)), _const_spec(wgu.shape), _const_spec(wd.shape),
                  _const_spec((1, d))],
        out_specs=row(d),
        out_shape=jax.ShapeDtypeStruct((b, s, d), F32),
        compiler_params=_params(("arbitrary", "arbitrary")),
        name="post_mix",
    )(x, ya, yb, gates, km, vm, wua, wub, wo, gc, wcq, wco, gf, wgu, wd, gfin)


def _rope_tables(seq):
    inv = 1.0 / (ROPE_THETA ** (jnp.arange(0, A_DK, 2, dtype=F32) / A_DK))
    ang = jnp.arange(seq, dtype=F32)[:, None] * inv[None, :]
    cos, sin = jnp.cos(ang), jnp.sin(ang)
    reps = LANES_V7X // A_DK
    cos_t = jnp.tile(jnp.concatenate([cos, cos], axis=1), (1, reps))
    sin_t = jnp.tile(jnp.concatenate([-sin, sin], axis=1), (1, reps))
    return cos_t, sin_t


def kernel(x, mem, norm_mix_g, w_in, lam_q1, lam_k1, lam_q2, lam_k2, subln_g, rel_bias, w_up_a, w_up_b, w_out, norm_cross_g, norm_mem_g, w_cq, w_ckv, w_co, norm_ffn_g, w_gate_up, w_down, norm_final_g):
    b, s, d = x.shape
    depth = w_in.shape[0]
    assert s % SEQ_TILE == 0 and s % IN_TILE == 0 and s % ATT_BLOCK == 0 and ATT_BLOCK == 4 * CHUNK
    assert B_LEFT_CHUNKS * CHUNK == 2 * ATT_BLOCK and B_MAX_REL <= ATT_BLOCK
    cos_t, sin_t = _rope_tables(s)
    row = lambda v: v.reshape(1, -1).astype(F32)
    for layer in range(depth):
        lam_init = 0.8 - 0.6 * math.exp(-0.3 * layer)
        km, vm = _mem_kv(mem, row(norm_mem_g[layer]), w_ckv[layer].astype(BF16))
        qkv, gates = _in_proj(x, row(norm_mix_g[layer]), w_in[layer].astype(BF16), cos_t, sin_t)
        ya = _diff_attn(qkv, row(lam_q1[layer]), row(lam_k1[layer]), row(lam_q2[layer]),
                        row(lam_k2[layer]), row(subln_g[layer]), lam_init)
        yb = _band_attn(qkv, _band_bias_lines(rel_bias[layer]))
        x = _post_mix(x, ya, yb, gates, km, vm,
                      w_up_a[layer].astype(BF16), w_up_b[layer].astype(BF16),
                      w_out[layer].astype(BF16), row(norm_cross_g[layer]),
                      w_cq[layer].astype(BF16), w_co[layer].astype(BF16),
                      row(norm_ffn_g[layer]), w_gate_up[layer].astype(BF16),
                      w_down[layer].astype(BF16), row(norm_final_g),
                      final_norm=(layer == depth - 1))
    return x
```

```python
import functools
import math

import jax
import jax.numpy as jnp
from jax import lax
from jax.experimental import pallas as pl
from jax.experimental.pallas import tpu as pltpu

CHUNK = 64
EPS = 1e-6
ROPE_THETA = 10000.0
NEG = -1e30

A_HEADS = 4
A_DK = 64
A_DV = 2 * A_DK
B_HEADS = 8
B_DH = 64
B_LEFT_CHUNKS = 8
B_MAX_REL = 256
X_HEADS = 4

LANES_V7X = 128
VMEM_LIMIT_BYTES_V7X = 56 * 1024 * 1024

IN_TILE = 1024
SEQ_TILE = 512
ATT_BLOCK = 256
FF_CHUNK = 512

BF16 = jnp.bfloat16
F32 = jnp.float32
LOG2E = math.log2(math.e)


def _rms(x, g):
    return x * lax.rsqrt(jnp.mean(x * x, axis=-1, keepdims=True) + EPS) * g


def _dot(a, b):
    return jnp.dot(a, b, preferred_element_type=F32)


def _dot_nt(a, b):
    return lax.dot_general(a, b, (((1,), (1,)), ((), ())), preferred_element_type=F32)


def _sigmoid(x):
    return 1.0 / (1.0 + jnp.exp(-x))


def _const_spec(shape):
    n = len(shape)
    return pl.BlockSpec(shape, lambda *_: (0,) * n, pipeline_mode=pl.Buffered(1))


def _params(semantics, flags=None):
    return pltpu.CompilerParams(dimension_semantics=semantics,
                                vmem_limit_bytes=VMEM_LIMIT_BYTES_V7X, flags=flags)


def _mem_kv_kernel(mem_ref, g_ref, w_ref, k_ref, v_ref):
    nb, nm, d = mem_ref.shape
    m = mem_ref[...].reshape(nb * nm, d)
    mn = _rms(m, g_ref[...]).astype(BF16)
    kv = _dot(mn, w_ref[...])
    k_ref[...] = kv[:, :d].astype(BF16).reshape(nb, nm, d)
    v_ref[...] = kv[:, d:].astype(BF16).reshape(nb, nm, d)


def _mem_kv(mem, g, w_ckv):
    b, nm, d = mem.shape
    nb = 2 if b % 2 == 0 else 1
    return pl.pallas_call(
        _mem_kv_kernel,
        grid=(b // nb,),
        in_specs=[pl.BlockSpec((nb, nm, d), lambda i: (i, 0, 0)),
                  _const_spec((1, d)),
                  _const_spec((d, 2 * d))],
        out_specs=[pl.BlockSpec((nb, nm, d), lambda i: (i, 0, 0))] * 2,
        out_shape=[jax.ShapeDtypeStruct((b, nm, d), BF16)] * 2,
        compiler_params=_params(("arbitrary",)),
        name="mem_kv",
    )(mem, g, w_ckv)


def _in_proj_kernel(x_ref, g_ref, w_ref, cos_ref, sin_ref, qkv_ref, gate_ref, *, q_scale):
    ts = x_ref.shape[1]
    h = _rms(x_ref[0], g_ref[...]).astype(BF16)
    cos = cos_ref[...]
    sin = sin_ref[...]
    lane = lax.broadcasted_iota(jnp.int32, (ts, LANES_V7X), 1)
    first_half = (lane % A_DK) < (A_DK // 2)
    slab_w = LANES_V7X
    grp_w = 4 * slab_w
    gate0 = 6 * grp_w
    for c in range(gate_ref.shape[2] // grp_w):
        z = _dot(h, w_ref[:, gate0 + c * grp_w: gate0 + (c + 1) * grp_w])
        gate_ref[0, :, c * grp_w:(c + 1) * grp_w] = _sigmoid(z).astype(BF16)
    for grp in (0, 1, 3, 2, 4, 5):
        z = _dot(h, w_ref[:, grp * grp_w:(grp + 1) * grp_w])
        for s in range(4):
            zs = z[:, s * slab_w:(s + 1) * slab_w]
            if grp < 2:
                partner = jnp.where(first_half,
                                    pltpu.roll(zs, LANES_V7X - A_DK // 2, 1),
                                    pltpu.roll(zs, A_DK // 2, 1))
                zs = zs * cos + partner * sin
            if grp in (0, 3):
                zs = zs * q_scale
            qkv_ref[0, grp * 4 + s] = zs.astype(BF16)


def _in_proj(x, g, w_in, cos_t, sin_t):
    b, s, d = x.shape
    ts = IN_TILE
    n_gate = w_in.shape[1] - 6 * 4 * LANES_V7X
    assert A_DK == B_DH
    kern = functools.partial(_in_proj_kernel, q_scale=A_DK ** -0.5 * LOG2E)
    return pl.pallas_call(
        kern,
        grid=(b, s // ts),
        in_specs=[pl.BlockSpec((1, ts, d), lambda bi, i: (bi, i, 0)),
                  _const_spec((1, d)),
                  _const_spec(w_in.shape),
                  pl.BlockSpec((ts, LANES_V7X), lambda bi, i: (i, 0)),
                  pl.BlockSpec((ts, LANES_V7X), lambda bi, i: (i, 0))],
        out_specs=[pl.BlockSpec((1, 24, ts, LANES_V7X), lambda bi, i: (bi, 0, i, 0)),
                   pl.BlockSpec((1, ts, n_gate), lambda bi, i: (bi, i, 0))],
        out_shape=[jax.ShapeDtypeStruct((b, 24, s, LANES_V7X), BF16),
                   jax.ShapeDtypeStruct((b, s, n_gate), BF16)],
        compiler_params=_params(("arbitrary", "arbitrary")),
        name="in_proj",
    )(x, g, w_in, cos_t, sin_t)


def _stack_halves(q, split):
    lane = lax.broadcasted_iota(jnp.int32, q.shape, 1)
    zero = jnp.zeros_like(q)
    return jnp.concatenate([jnp.where(lane < split, q, zero),
                            jnp.where(lane >= split, q, zero)], axis=0)


ONES_ROWS = 16


def _load_vt(v_ref, vt_ref):
    w, seq = v_ref.shape[3], v_ref.shape[2]
    vt_ref[:w, :] = v_ref[0, 0].T
    vt_ref[w:, :] = jnp.ones((ONES_ROWS, seq), BF16)


def _row_softmax_pv(s, v):
    p = jnp.exp(s - jnp.max(s, axis=1, keepdims=True))
    return _dot(p.astype(BF16), v), jnp.sum(p, axis=1, keepdims=True)


def _diff_attn_kernel(q_ref, k_ref, v_ref, lq1_ref, lk1_ref, lq2_ref, lk2_ref, g_ref,
                      o_ref, vt_ref, s0_ref, s1_ref, p0_ref, p1_ref, *, lam_init):
    tq = ATT_BLOCK
    seq, w = q_ref.shape[2], q_ref.shape[3]
    _load_vt(v_ref, vt_ref)
    lam = (jnp.exp(jnp.sum(lq1_ref[...] * lk1_ref[...], axis=1, keepdims=True))
           - jnp.exp(jnp.sum(lq2_ref[...] * lk2_ref[...], axis=1, keepdims=True))
           + lam_init)
    key = lax.broadcasted_iota(jnp.int32, (tq, 2 * tq), 0)
    qry = lax.broadcasted_iota(jnp.int32, (tq, 2 * tq), 1)
    allowed = (key // CHUNK) <= ((qry % tq) // CHUNK)

    s_refs, p_refs = (s0_ref, s1_ref), (p0_ref, p1_ref)

    def scores(qi):
        qq = _stack_halves(q_ref[0, 0, qi * tq:(qi + 1) * tq, :], A_DK)
        m8 = None
        for kj in range(qi + 1):
            rows = slice(kj * tq, (kj + 1) * tq)
            s = _dot_nt(k_ref[0, 0, rows, :], qq)
            if kj == qi:
                s = jnp.where(allowed, s, NEG)
            s_refs[qi % 2][rows, :] = s
            mj = jnp.max(s.reshape(tq // 8, 8, 2 * tq), axis=0)
            m8 = mj if m8 is None else jnp.maximum(m8, mj)
        return jnp.max(m8, axis=0, keepdims=True)

    def probs(qi, m):
        for kj in range(qi + 1):
            rows = slice(kj * tq, (kj + 1) * tq)
            p = jnp.exp2(s_refs[qi % 2][rows, :] - m)
            p_refs[qi % 2][rows, :] = p.astype(BF16)

    def values(qi):
        q0, q1 = qi * tq, (qi + 1) * tq
        ot = _dot(vt_ref[:, :q1], p_refs[qi % 2][:q1, :])
        ot = ot[:w] / ot[w:w + 1]
        o = (ot[:, :tq] - lam * ot[:, tq:]).T
        o = _rms(o, g_ref[...]) * (1.0 - lam_init)
        o_ref[0, q0:q1, :] = o.astype(BF16)

    nq = seq // tq
    m_cur = scores(0)
    for qi in range(nq):
        m_next = scores(qi + 1) if qi + 1 < nq else None
        if qi:
            values(qi - 1)
        probs(qi, m_cur)
        m_cur = m_next
    values(nq - 1)


def _diff_attn(qkv, lq1, lk1, lq2, lk2, subln_g, lam_init):
    b, _, s, w = qkv.shape
    kern = functools.partial(_diff_attn_kernel, lam_init=lam_init)
    vec = _const_spec((1, A_DK))
    slab = lambda g0: pl.BlockSpec((1, 1, s, w), lambda bi, h: (bi, g0 + h, 0, 0))
    return pl.pallas_call(
        kern,
        grid=(b, A_HEADS),
        in_specs=[slab(0), slab(4), slab(8), vec, vec, vec, vec, _const_spec((1, A_DV))],
        out_specs=pl.BlockSpec((1, s, w), lambda bi, h: (bi, 0, h)),
        out_shape=jax.ShapeDtypeStruct((b, s, A_HEADS * A_DV), BF16),
        scratch_shapes=[pltpu.VMEM((w + ONES_ROWS, s), BF16),
                        pltpu.VMEM((s, 2 * ATT_BLOCK), F32), pltpu.VMEM((s, 2 * ATT_BLOCK), F32),
                        pltpu.VMEM((s, 2 * ATT_BLOCK), BF16), pltpu.VMEM((s, 2 * ATT_BLOCK), BF16)],
        compiler_params=_params(("arbitrary", "arbitrary")),
        name="diff_attn",
    )(qkv, qkv, qkv, lq1, lk1, lq2, lk2, subln_g)


def _band_tiles_init(bias_ref, tab_ref):
    t = ATT_BLOCK
    kc = lax.broadcasted_iota(jnp.int32, (t, t), 0) // CHUNK
    qc = lax.broadcasted_iota(jnp.int32, (t, t), 1) // CHUNK
    for head in range(2):
        cols = slice(head * t, (head + 1) * t)
        for back in range(2):
            line = jnp.broadcast_to(bias_ref[0, head, back:back + 1, :], (t, 2 * t))
            tile = pltpu.roll(line, 0, 1, stride=1, stride_axis=0)[:, :t]
            tab_ref[back, :, cols] = jnp.where(kc <= qc, tile, NEG) if back == 0 else tile
        far = jnp.broadcast_to(bias_ref[0, head, 1:2, 0:1], (t, t))
        tab_ref[2, :, cols] = jnp.where(kc >= qc, far, NEG)


def _band_attn_kernel(q_ref, k_ref, v_ref, bias_ref, o_ref, tab_ref, vt_ref,
                      s0_ref, s1_ref, p0_ref, p1_ref):
    tq = ATT_BLOCK
    seq, w = q_ref.shape[2], q_ref.shape[3]

    @pl.when(pl.program_id(1) == 0)
    def _():
        _band_tiles_init(bias_ref, tab_ref)

    _load_vt(v_ref, vt_ref)
    first_head = lax.broadcasted_iota(jnp.int32, (LANES_V7X, tq), 0) < B_DH
    s_refs, p_refs = (s0_ref, s1_ref), (p0_ref, p1_ref)

    def scores(qi):
        qq = _stack_halves(q_ref[0, 0, qi * tq:(qi + 1) * tq, :], B_DH)
        m8 = None
        for back in range(min(qi, 2) + 1):
            kj = qi - back
            s = _dot_nt(k_ref[0, 0, kj * tq:(kj + 1) * tq, :], qq) + tab_ref[back]
            s_refs[qi % 2][back * tq:(back + 1) * tq, :] = s
            mj = jnp.max(s.reshape(tq // 8, 8, 2 * tq), axis=0)
            m8 = mj if m8 is None else jnp.maximum(m8, mj)
        return jnp.max(m8, axis=0, keepdims=True)

    def probs(qi, m):
        nblk = min(qi, 2) + 1
        for back in range(nblk):
            p = jnp.exp2(s_refs[qi % 2][back * tq:(back + 1) * tq, :] - m)
            pos = nblk - 1 - back
            p_refs[qi % 2][pos * tq:(pos + 1) * tq, :] = p.astype(BF16)

    def values(qi):
        q0, q1 = qi * tq, (qi + 1) * tq
        nblk = min(qi, 2) + 1
        ot = _dot(vt_ref[:, q1 - nblk * tq:q1], p_refs[qi % 2][:nblk * tq, :])
        ot = ot[:w] / ot[w:w + 1]
        o = jnp.where(first_head, ot[:, :tq], ot[:, tq:]).T
        o_ref[0, q0:q1, :] = o.astype(BF16)

    nq = seq // tq
    m_cur = scores(0)
    for qi in range(nq):
        m_next = scores(qi + 1) if qi + 1 < nq else None
        probs(qi, m_cur)
        values(qi)
        m_cur = m_next


def _band_bias_lines(rel_bias):
    t = ATT_BLOCK
    u = jnp.arange(2 * t)
    u = jnp.where(u < t, u, u - 2 * t)
    idx = jnp.stack([jnp.clip(u, -B_MAX_REL, B_MAX_REL), jnp.clip(t + u, -B_MAX_REL, B_MAX_REL)])
    lines = rel_bias[:, idx + B_MAX_REL].astype(F32) * LOG2E
    return lines.reshape(B_HEADS // 2, 2, 2, 2 * t)


def _band_attn(qkv, lines):
    b, _, s, w = qkv.shape
    tq = ATT_BLOCK
    npair = B_HEADS // 2
    slab = lambda g0: pl.BlockSpec((1, 1, s, w), lambda j, bi: (bi, g0 + j, 0, 0))
    return pl.pallas_call(
        _band_attn_kernel,
        grid=(npair, b),
        in_specs=[slab(12), slab(16), slab(20),
                  pl.BlockSpec((1, 2, 2, 2 * tq), lambda j, bi: (j, 0, 0, 0))],
        out_specs=pl.BlockSpec((1, s, w), lambda j, bi: (bi, 0, j)),
        out_shape=jax.ShapeDtypeStruct((b, s, B_HEADS * B_DH), BF16),
        scratch_shapes=[pltpu.VMEM((3, tq, 2 * tq), F32), pltpu.VMEM((w + ONES_ROWS, s), BF16),
                        pltpu.VMEM((3 * tq, 2 * tq), F32), pltpu.VMEM((3 * tq, 2 * tq), F32),
                        pltpu.VMEM((3 * tq, 2 * tq), BF16), pltpu.VMEM((3 * tq, 2 * tq), BF16)],
        compiler_params=_params(("arbitrary", "arbitrary")),
        name="band_attn",
    )(qkv, qkv, qkv, lines)


def _post_mix_kernel(x_ref, ya_ref, yb_ref, gate_ref, km_ref, vm_ref,
                     wua_ref, wub_ref, wo_ref, gc_ref, wcq_ref, wco_ref,
                     gf_ref, wgu_ref, wd_ref, gfin_ref, o_ref, *, final_norm):
    d = x_ref.shape[2]
    ua = _dot(ya_ref[0], wua_ref[...])
    ub = _dot(yb_ref[0], wub_ref[...])
    merged = (gate_ref[0, :, :d].astype(F32) * ua + gate_ref[0, :, d:].astype(F32) * ub)
    x = x_ref[0] + _dot(merged.astype(BF16), wo_ref[...])

    dh = d // X_HEADS
    hc = _rms(x, gc_ref[...]).astype(BF16)
    q = (_dot(hc, wcq_ref[...]) * (dh ** -0.5)).astype(BF16)
    heads = []
    for h in range(X_HEADS):
        sl = slice(h * dh, (h + 1) * dh)
        pv, l = _row_softmax_pv(_dot_nt(q[:, sl], km_ref[0, :, sl]), vm_ref[0, :, sl])
        heads.append((pv / l).astype(BF16))
    x = x + _dot(jnp.concatenate(heads, axis=1), wco_ref[...])

    hf = _rms(x, gf_ref[...]).astype(BF16)
    dff = wd_ref.shape[0]
    acc = jnp.zeros_like(x)
    for c0 in range(0, dff, FF_CHUNK):
        c1 = min(c0 + FF_CHUNK, dff)
        g = _dot(hf, wgu_ref[:, c0:c1])
        u = _dot(hf, wgu_ref[:, dff + c0:dff + c1])
        a = (g * _sigmoid(g) * u).astype(BF16)
        acc = acc + _dot(a, wd_ref[c0:c1, :])
    x = x + acc
    if final_norm:
        x = _rms(x, gfin_ref[...])
    o_ref[0] = x


def _post_mix(x, ya, yb, gates, km, vm, wua, wub, wo, gc, wcq, wco, gf, wgu, wd, gfin,
              final_norm):
    b, s, d = x.shape
    ts = SEQ_TILE
    nm = km.shape[1]
    row = lambda w: pl.BlockSpec((1, ts, w), lambda bi, i: (bi, i, 0))
    mem = pl.BlockSpec((1, nm, d), lambda bi, i: (bi, 0, 0))
    kern = functools.partial(_post_mix_kernel, final_norm=final_norm)
    return pl.pallas_call(
        kern,
        grid=(b, s // ts),
        in_specs=[row(d), row(ya.shape[2]), row(yb.shape[2]), row(gates.shape[2]), mem, mem,
                  _const_spec(wua.shape), _const_spec(wub.shape), _const_spec(wo.shape),
                  _const_spec((1, d)), _const_spec(wcq.shape), _const_spec(wco.shape),
                  _const_spec((1, d)), _const_spec(wgu.shape), _const_spec(wd.shape),
                  _const_spec((1, d))],
        out_specs=row(d),
        out_shape=jax.ShapeDtypeStruct((b, s, d), F32),
        compiler_params=_params(("arbitrary", "arbitrary")),
        name="post_mix",
    )(x, ya, yb, gates, km, vm, wua, wub, wo, gc, wcq, wco, gf, wgu, wd, gfin)


def _rope_tables(seq):
    inv = 1.0 / (ROPE_THETA ** (jnp.arange(0, A_DK, 2, dtype=F32) / A_DK))
    ang = jnp.arange(seq, dtype=F32)[:, None] * inv[None, :]
    cos, sin = jnp.cos(ang), jnp.sin(ang)
    reps = LANES_V7X // A_DK
    cos_t = jnp.tile(jnp.concatenate([cos, cos], axis=1), (1, reps))
    sin_t = jnp.tile(jnp.concatenate([-sin, sin], axis=1), (1, reps))
    return cos_t, sin_t


def kernel(x, mem, norm_mix_g, w_in, lam_q1, lam_k1, lam_q2, lam_k2, subln_g, rel_bias, w_up_a, w_up_b, w_out, norm_cross_g, norm_mem_g, w_cq, w_ckv, w_co, norm_ffn_g, w_gate_up, w_down, norm_final_g):
    b, s, d = x.shape
    depth = w_in.shape[0]
    assert s % SEQ_TILE == 0 and s % IN_TILE == 0 and s % ATT_BLOCK == 0 and ATT_BLOCK == 4 * CHUNK
    assert B_LEFT_CHUNKS * CHUNK == 2 * ATT_BLOCK and B_MAX_REL <= ATT_BLOCK
    cos_t, sin_t = _rope_tables(s)
    row = lambda v: v.reshape(1, -1).astype(F32)
    for layer in range(depth):
        lam_init = 0.8 - 0.6 * math.exp(-0.3 * layer)
        km, vm = _mem_kv(mem, row(norm_mem_g[layer]), w_ckv[layer].astype(BF16))
        qkv, gates = _in_proj(x, row(norm_mix_g[layer]), w_in[layer].astype(BF16), cos_t, sin_t)
        ya = _diff_attn(qkv, row(lam_q1[layer]), row(lam_k1[layer]), row(lam_q2[layer]),
                        row(lam_k2[layer]), row(subln_g[layer]), lam_init)
        yb = _band_attn(qkv, _band_bias_lines(rel_bias[layer]))
        x = _post_mix(x, ya, yb, gates, km, vm,
                      w_up_a[layer].astype(BF16), w_up_b[layer].astype(BF16),
                      w_out[layer].astype(BF16), row(norm_cross_g[layer]),
                      w_cq[layer].astype(BF16), w_co[layer].astype(BF16),
                      row(norm_ffn_g[layer]), w_gate_up[layer].astype(BF16),
                      w_down[layer].astype(BF16), row(norm_final_g),
                      final_norm=(layer == depth - 1))
    return x
```

```python
import functools
import math

import jax
import jax.numpy as jnp
from jax import lax
from jax.experimental import pallas as pl
from jax.experimental.pallas import tpu as pltpu

CHUNK = 64
EPS = 1e-6
ROPE_THETA = 10000.0
NEG = -1e30

A_HEADS = 4
A_DK = 64
A_DV = 2 * A_DK
B_HEADS = 8
B_DH = 64
B_LEFT_CHUNKS = 8
B_MAX_REL = 256
X_HEADS = 4

LANES_V7X = 128
VMEM_LIMIT_BYTES_V7X = 56 * 1024 * 1024

IN_TILE = 1024
SEQ_TILE = 512
ATT_BLOCK = 256
FF_CHUNK = 512

BF16 = jnp.bfloat16
F32 = jnp.float32
LOG2E = math.log2(math.e)


def _rms(x, g):
    return x * lax.rsqrt(jnp.mean(x * x, axis=-1, keepdims=True) + EPS) * g


def _dot(a, b):
    return jnp.dot(a, b, preferred_element_type=F32)


def _dot_nt(a, b):
    return lax.dot_general(a, b, (((1,), (1,)), ((), ())), preferred_element_type=F32)


def _sigmoid(x):
    return 1.0 / (1.0 + jnp.exp(-x))


def _const_spec(shape):
    n = len(shape)
    return pl.BlockSpec(shape, lambda *_: (0,) * n, pipeline_mode=pl.Buffered(1))


def _params(semantics, flags=None):
    return pltpu.CompilerParams(dimension_semantics=semantics,
                                vmem_limit_bytes=VMEM_LIMIT_BYTES_V7X, flags=flags)


def _mem_kv_kernel(mem_ref, g_ref, w_ref, k_ref, v_ref):
    nb, nm, d = mem_ref.shape
    m = mem_ref[...].reshape(nb * nm, d)
    mn = _rms(m, g_ref[...]).astype(BF16)
    kv = _dot(mn, w_ref[...])
    k_ref[...] = kv[:, :d].astype(BF16).reshape(nb, nm, d)
    v_ref[...] = kv[:, d:].astype(BF16).reshape(nb, nm, d)


def _mem_kv(mem, g, w_ckv):
    b, nm, d = mem.shape
    nb = 2 if b % 2 == 0 else 1
    return pl.pallas_call(
        _mem_kv_kernel,
        grid=(b // nb,),
        in_specs=[pl.BlockSpec((nb, nm, d), lambda i: (i, 0, 0)),
                  _const_spec((1, d)),
                  _const_spec((d, 2 * d))],
        out_specs=[pl.BlockSpec((nb, nm, d), lambda i: (i, 0, 0))] * 2,
        out_shape=[jax.ShapeDtypeStruct((b, nm, d), BF16)] * 2,
        compiler_params=_params(("arbitrary",)),
        name="mem_kv",
    )(mem, g, w_ckv)


def _in_proj_kernel(x_ref, g_ref, w_ref, cos_ref, sin_ref, qkv_ref, gate_ref, *, q_scale):
    ts = x_ref.shape[1]
    h = _rms(x_ref[0], g_ref[...]).astype(BF16)
    cos = cos_ref[...]
    sin = sin_ref[...]
    lane = lax.broadcasted_iota(jnp.int32, (ts, LANES_V7X), 1)
    first_half = (lane % A_DK) < (A_DK // 2)
    slab_w = LANES_V7X
    grp_w = 4 * slab_w
    gate0 = 6 * grp_w
    for c in range(gate_ref.shape[2] // grp_w):
        z = _dot(h, w_ref[:, gate0 + c * grp_w: gate0 + (c + 1) * grp_w])
        gate_ref[0, :, c * grp_w:(c + 1) * grp_w] = _sigmoid(z).astype(BF16)
    for grp in (0, 1, 3, 2, 4, 5):
        z = _dot(h, w_ref[:, grp * grp_w:(grp + 1) * grp_w])
        for s in range(4):
            zs = z[:, s * slab_w:(s + 1) * slab_w]
            if grp < 2:
                partner = jnp.where(first_half,
                                    pltpu.roll(zs, LANES_V7X - A_DK // 2, 1),
                                    pltpu.roll(zs, A_DK // 2, 1))
                zs = zs * cos + partner * sin
            if grp in (0, 3):
                zs = zs * q_scale
            qkv_ref[0, grp * 4 + s] = zs.astype(BF16)


def _in_proj(x, g, w_in, cos_t, sin_t):
    b, s, d = x.shape
    ts = IN_TILE
    n_gate = w_in.shape[1] - 6 * 4 * LANES_V7X
    assert A_DK == B_DH
    kern = functools.partial(_in_proj_kernel, q_scale=A_DK ** -0.5 * LOG2E)
    return pl.pallas_call(
        kern,
        grid=(b, s // ts),
        in_specs=[pl.BlockSpec((1, ts, d), lambda bi, i: (bi, i, 0)),
                  _const_spec((1, d)),
                  _const_spec(w_in.shape),
                  pl.BlockSpec((ts, LANES_V7X), lambda bi, i: (i, 0)),
                  pl.BlockSpec((ts, LANES_V7X), lambda bi, i: (i, 0))],
        out_specs=[pl.BlockSpec((1, 24, ts, LANES_V7X), lambda bi, i: (bi, 0, i, 0)),
                   pl.BlockSpec((1, ts, n_gate), lambda bi, i: (bi, i, 0))],
        out_shape=[jax.ShapeDtypeStruct((b, 24, s, LANES_V7X), BF16),
                   jax.ShapeDtypeStruct((b, s, n_gate), BF16)],
        compiler_params=_params(("arbitrary", "arbitrary")),
        name="in_proj",
    )(x, g, w_in, cos_t, sin_t)


def _stack_halves(q, split):
    lane = lax.broadcasted_iota(jnp.int32, q.shape, 1)
    zero = jnp.zeros_like(q)
    return jnp.concatenate([jnp.where(lane < split, q, zero),
                            jnp.where(lane >= split, q, zero)], axis=0)


ONES_ROWS = 16


def _load_vt(v_ref, vt_ref):
    w, seq = v_ref.shape[3], v_ref.shape[2]
    vt_ref[:w, :] = v_ref[0, 0].T
    vt_ref[w:, :] = jnp.ones((ONES_ROWS, seq), BF16)


def _row_softmax_pv(s, v):
    p = jnp.exp(s - jnp.max(s, axis=1, keepdims=True))
    return _dot(p.astype(BF16), v), jnp.sum(p, axis=1, keepdims=True)


def _diff_attn_kernel(q_ref, k_ref, v_ref, lq1_ref, lk1_ref, lq2_ref, lk2_ref, g_ref,
                      o_ref, vt_ref, s0_ref, s1_ref, p0_ref, p1_ref, *, lam_init):
    tq = ATT_BLOCK
    seq, w = q_ref.shape[2], q_ref.shape[3]
    _load_vt(v_ref, vt_ref)
    lam = (jnp.exp(jnp.sum(lq1_ref[...] * lk1_ref[...], axis=1, keepdims=True))
           - jnp.exp(jnp.sum(lq2_ref[...] * lk2_ref[...], axis=1, keepdims=True))
           + lam_init)
    key = lax.broadcasted_iota(jnp.int32, (tq, 2 * tq), 0)
    qry = lax.broadcasted_iota(jnp.int32, (tq, 2 * tq), 1)
    allowed = (key // CHUNK) <= ((qry % tq) // CHUNK)

    s_refs, p_refs = (s0_ref, s1_ref), (p0_ref, p1_ref)

    def scores(slot, qi):
        qq = _stack_halves(q_ref[0, 0, qi * tq:(qi + 1) * tq, :], A_DK)
        m8 = None
        for kj in range(qi + 1):
            rows = slice(kj * tq, (kj + 1) * tq)
            s = _dot_nt(k_ref[0, 0, rows, :], qq)
            if kj == qi:
                s = jnp.where(allowed, s, NEG)
            s_refs[slot][rows, :] = s
            mj = jnp.max(s.reshape(tq // 8, 8, 2 * tq), axis=0)
            m8 = mj if m8 is None else jnp.maximum(m8, mj)
        return jnp.max(m8, axis=0, keepdims=True)

    def probs(slot, qi, m):
        for kj in range(qi + 1):
            rows = slice(kj * tq, (kj + 1) * tq)
            p = jnp.exp2(s_refs[slot][rows, :] - m)
            p_refs[slot][rows, :] = p.astype(BF16)

    def values(slot, qi):
        q0, q1 = qi * tq, (qi + 1) * tq
        ot = _dot(vt_ref[:, :q1], p_refs[slot][:q1, :])
        ot = ot[:w] / ot[w:w + 1]
        o = (ot[:, :tq] - lam * ot[:, tq:]).T
        o = _rms(o, g_ref[...]) * (1.0 - lam_init)
        o_ref[0, q0:q1, :] = o.astype(BF16)

    nq = seq // tq
    order = list(range(nq))
    m_cur = scores(0, order[0])
    for pos, qi in enumerate(order):
        m_next = scores((pos + 1) % 2, order[pos + 1]) if pos + 1 < nq else None
        if pos:
            values((pos - 1) % 2, order[pos - 1])
        probs(pos % 2, qi, m_cur)
        m_cur = m_next
    values((nq - 1) % 2, order[-1])


def _band_tiles_init(bias_ref, tab_ref):
    t = ATT_BLOCK
    kc = lax.broadcasted_iota(jnp.int32, (t, t), 0) // CHUNK
    qc = lax.broadcasted_iota(jnp.int32, (t, t), 1) // CHUNK
    for head in range(2):
        cols = slice(head * t, (head + 1) * t)
        for back in range(2):
            line = jnp.broadcast_to(bias_ref[0, head, back:back + 1, :], (t, 2 * t))
            tile = pltpu.roll(line, 0, 1, stride=1, stride_axis=0)[:, :t]
            tab_ref[back, :, cols] = jnp.where(kc <= qc, tile, NEG) if back == 0 else tile
        far = jnp.broadcast_to(bias_ref[0, head, 1:2, 0:1], (t, t))
        tab_ref[2, :, cols] = jnp.where(kc >= qc, far, NEG)


def _band_attn_kernel(q_ref, k_ref, v_ref, bias_ref, o_ref, tab_ref, vt_ref,
                      s0_ref, s1_ref, p0_ref, p1_ref):
    tq = ATT_BLOCK
    seq, w = q_ref.shape[2], q_ref.shape[3]
    _load_vt(v_ref, vt_ref)
    first_head = lax.broadcasted_iota(jnp.int32, (LANES_V7X, tq), 0) < B_DH
    s_refs, p_refs = (s0_ref, s1_ref), (p0_ref, p1_ref)

    def scores(qi):
        qq = _stack_halves(q_ref[0, 0, qi * tq:(qi + 1) * tq, :], B_DH)
        m8 = None
        for back in range(min(qi, 2) + 1):
            kj = qi - back
            s = _dot_nt(k_ref[0, 0, kj * tq:(kj + 1) * tq, :], qq) + tab_ref[back]
            s_refs[qi % 2][back * tq:(back + 1) * tq, :] = s
            mj = jnp.max(s.reshape(tq // 8, 8, 2 * tq), axis=0)
            m8 = mj if m8 is None else jnp.maximum(m8, mj)
        return jnp.max(m8, axis=0, keepdims=True)

    def probs(qi, m):
        nblk = min(qi, 2) + 1
        for back in range(nblk):
            p = jnp.exp2(s_refs[qi % 2][back * tq:(back + 1) * tq, :] - m)
            pos = nblk - 1 - back
            p_refs[qi % 2][pos * tq:(pos + 1) * tq, :] = p.astype(BF16)

    def values(qi):
        q0, q1 = qi * tq, (qi + 1) * tq
        nblk = min(qi, 2) + 1
        ot = _dot(vt_ref[:, q1 - nblk * tq:q1], p_refs[qi % 2][:nblk * tq, :])
        ot = ot[:w] / ot[w:w + 1]
        o = jnp.where(first_head, ot[:, :tq], ot[:, tq:]).T
        o_ref[0, q0:q1, :] = o.astype(BF16)

    nq = seq // tq
    m_cur = scores(0)
    for qi in range(nq):
        m_next = scores(qi + 1) if qi + 1 < nq else None
        probs(qi, m_cur)
        values(qi)
        m_cur = m_next


def _band_bias_lines(rel_bias):
    t = ATT_BLOCK
    u = jnp.arange(2 * t)
    u = jnp.where(u < t, u, u - 2 * t)
    idx = jnp.stack([jnp.clip(u, -B_MAX_REL, B_MAX_REL), jnp.clip(t + u, -B_MAX_REL, B_MAX_REL)])
    lines = rel_bias[:, idx + B_MAX_REL].astype(F32) * LOG2E
    return lines.reshape(B_HEADS // 2, 2, 2, 2 * t)


def _attn_kernel(qa_ref, ka_ref, va_ref, lq1_ref, lk1_ref, lq2_ref, lk2_ref, g_ref,
                 qb_ref, kb_ref, vb_ref, bias_ref, ya_ref, yb_ref,
                 vta_ref, sa0_ref, sa1_ref, pa0_ref, pa1_ref,
                 tab_ref, vtb_ref, sb0_ref, sb1_ref, pb0_ref, pb1_ref, *, lam_init):
    pair_tab = tab_ref.at[pl.program_id(1)]

    @pl.when(pl.program_id(0) == 0)
    def _():
        _band_tiles_init(bias_ref, pair_tab)

    _diff_attn_kernel(qa_ref, ka_ref, va_ref, lq1_ref, lk1_ref, lq2_ref, lk2_ref, g_ref,
                      ya_ref, vta_ref, sa0_ref, sa1_ref, pa0_ref, pa1_ref, lam_init=lam_init)
    _band_attn_kernel(qb_ref, kb_ref, vb_ref, bias_ref, yb_ref, pair_tab, vtb_ref,
                      sb0_ref, sb1_ref, pb0_ref, pb1_ref)


def _attn(qkv, lq1, lk1, lq2, lk2, subln_g, lines, lam_init):
    b, _, s, w = qkv.shape
    tq = ATT_BLOCK
    npair = B_HEADS // 2
    assert npair == A_HEADS
    kern = functools.partial(_attn_kernel, lam_init=lam_init)
    vec = _const_spec((1, A_DK))
    slab = lambda g0: pl.BlockSpec((1, 1, s, w), lambda bi, h: (bi, g0 + h, 0, 0))
    out = pl.BlockSpec((1, s, w), lambda bi, h: (bi, 0, h))
    return pl.pallas_call(
        kern,
        grid=(b, A_HEADS),
        in_specs=[slab(0), slab(4), slab(8), vec, vec, vec, vec, _const_spec((1, A_DV)),
                  slab(12), slab(16), slab(20),
                  pl.BlockSpec((1, 2, 2, 2 * tq), lambda bi, h: (h, 0, 0, 0))],
        out_specs=[out, out],
        out_shape=[jax.ShapeDtypeStruct((b, s, A_HEADS * A_DV), BF16),
                   jax.ShapeDtypeStruct((b, s, B_HEADS * B_DH), BF16)],
        scratch_shapes=[pltpu.VMEM((w + ONES_ROWS, s), BF16),
                        pltpu.VMEM((s, 2 * tq), F32), pltpu.VMEM((s, 2 * tq), F32),
                        pltpu.VMEM((s, 2 * tq), BF16), pltpu.VMEM((s, 2 * tq), BF16),
                        pltpu.VMEM((npair, 3, tq, 2 * tq), F32),
                        pltpu.VMEM((w + ONES_ROWS, s), BF16),
                        pltpu.VMEM((3 * tq, 2 * tq), F32), pltpu.VMEM((3 * tq, 2 * tq), F32),
                        pltpu.VMEM((3 * tq, 2 * tq), BF16), pltpu.VMEM((3 * tq, 2 * tq), BF16)],
        compiler_params=_params(("arbitrary", "arbitrary")),
        name="attn",
    )(qkv, qkv, qkv, lq1, lk1, lq2, lk2, subln_g, qkv, qkv, qkv, lines)


def _post_mix_kernel(x_ref, ya_ref, yb_ref, gate_ref, km_ref, vm_ref,
                     wua_ref, wub_ref, wo_ref, gc_ref, wcq_ref, wco_ref,
                     gf_ref, wgu_ref, wd_ref, gfin_ref, o_ref, *, final_norm):
    d = x_ref.shape[2]
    ua = _dot(ya_ref[0], wua_ref[...])
    ub = _dot(yb_ref[0], wub_ref[...])
    merged = (gate_ref[0, :, :d].astype(F32) * ua + gate_ref[0, :, d:].astype(F32) * ub)
    x = x_ref[0] + _dot(merged.astype(BF16), wo_ref[...])

    dh = d // X_HEADS
    hc = _rms(x, gc_ref[...]).astype(BF16)
    q = (_dot(hc, wcq_ref[...]) * (dh ** -0.5)).astype(BF16)
    heads = []
    for h in range(X_HEADS):
        sl = slice(h * dh, (h + 1) * dh)
        pv, l = _row_softmax_pv(_dot_nt(q[:, sl], km_ref[0, :, sl]), vm_ref[0, :, sl])
        heads.append((pv / l).astype(BF16))
    x = x + _dot(jnp.concatenate(heads, axis=1), wco_ref[...])

    hf = _rms(x, gf_ref[...]).astype(BF16)
    dff = wd_ref.shape[0]
    acc = jnp.zeros_like(x)
    for c0 in range(0, dff, FF_CHUNK):
        c1 = min(c0 + FF_CHUNK, dff)
        g = _dot(hf, wgu_ref[:, c0:c1])
        u = _dot(hf, wgu_ref[:, dff + c0:dff + c1])
        a = (g * _sigmoid(g) * u).astype(BF16)
        acc = acc + _dot(a, wd_ref[c0:c1, :])
    x = x + acc
    if final_norm:
        x = _rms(x, gfin_ref[...])
    o_ref[0] = x


def _post_mix(x, ya, yb, gates, km, vm, wua, wub, wo, gc, wcq, wco, gf, wgu, wd, gfin,
              final_norm):
    b, s, d = x.shape
    ts = SEQ_TILE
    nm = km.shape[1]
    row = lambda w: pl.BlockSpec((1, ts, w), lambda bi, i: (bi, i, 0))
    mem = pl.BlockSpec((1, nm, d), lambda bi, i: (bi, 0, 0))
    kern = functools.partial(_post_mix_kernel, final_norm=final_norm)
    return pl.pallas_call(
        kern,
        grid=(b, s // ts),
        in_specs=[row(d), row(ya.shape[2]), row(yb.shape[2]), row(gates.shape[2]), mem, mem,
                  _const_spec(wua.shape), _const_spec(wub.shape), _const_spec(wo.shape),
                  _const_spec((1, d)), _const_spec(wcq.shape), _const_spec(wco.shape),
                  _const_spec((1, d)), _const_spec(wgu.shape), _const_spec(wd.shape),
                  _const_spec((1, d))],
        out_specs=row(d),
        out_shape=jax.ShapeDtypeStruct((b, s, d), F32),
        compiler_params=_params(("arbitrary", "arbitrary")),
        name="post_mix",
    )(x, ya, yb, gates, km, vm, wua, wub, wo, gc, wcq, wco, gf, wgu, wd, gfin)


def _rope_tables(seq):
    inv = 1.0 / (ROPE_THETA ** (jnp.arange(0, A_DK, 2, dtype=F32) / A_DK))
    ang = jnp.arange(seq, dtype=F32)[:, None] * inv[None, :]
    cos, sin = jnp.cos(ang), jnp.sin(ang)
    reps = LANES_V7X // A_DK
    cos_t = jnp.tile(jnp.concatenate([cos, cos], axis=1), (1, reps))
    sin_t = jnp.tile(jnp.concatenate([-sin, sin], axis=1), (1, reps))
    return cos_t, sin_t


def kernel(x, mem, norm_mix_g, w_in, lam_q1, lam_k1, lam_q2, lam_k2, subln_g, rel_bias, w_up_a, w_up_b, w_out, norm_cross_g, norm_mem_g, w_cq, w_ckv, w_co, norm_ffn_g, w_gate_up, w_down, norm_final_g):
    b, s, d = x.shape
    depth = w_in.shape[0]
    assert s % SEQ_TILE == 0 and s % IN_TILE == 0 and s % ATT_BLOCK == 0 and ATT_BLOCK == 4 * CHUNK
    assert B_LEFT_CHUNKS * CHUNK == 2 * ATT_BLOCK and B_MAX_REL <= ATT_BLOCK
    cos_t, sin_t = _rope_tables(s)
    row = lambda v: v.reshape(1, -1).astype(F32)
    for layer in range(depth):
        lam_init = 0.8 - 0.6 * math.exp(-0.3 * layer)
        km, vm = _mem_kv(mem, row(norm_mem_g[layer]), w_ckv[layer].astype(BF16))
        qkv, gates = _in_proj(x, row(norm_mix_g[layer]), w_in[layer].astype(BF16), cos_t, sin_t)
        ya, yb = _attn(qkv, row(lam_q1[layer]), row(lam_k1[layer]), row(lam_q2[layer]),
                       row(lam_k2[layer]), row(subln_g[layer]),
                       _band_bias_lines(rel_bias[layer]), lam_init)
        x = _post_mix(x, ya, yb, gates, km, vm,
                      w_up_a[layer].astype(BF16), w_up_b[layer].astype(BF16),
                      w_out[layer].astype(BF16), row(norm_cross_g[layer]),
                      w_cq[layer].astype(BF16), w_co[layer].astype(BF16),
                      row(norm_ffn_g[layer]), w_gate_up[layer].astype(BF16),
                      w_down[layer].astype(BF16), row(norm_final_g),
                      final_norm=(layer == depth - 1))
    return x
```

```python
import functools
import math

import jax
import jax.numpy as jnp
from jax import lax
from jax.experimental import pallas as pl
from jax.experimental.pallas import tpu as pltpu

CHUNK = 64
EPS = 1e-6
ROPE_THETA = 10000.0
NEG = -1e30

A_HEADS = 4
A_DK = 64
A_DV = 2 * A_DK
B_HEADS = 8
B_DH = 64
B_LEFT_CHUNKS = 8
B_MAX_REL = 256
X_HEADS = 4

LANES_V7X = 128
VMEM_LIMIT_BYTES_V7X = 56 * 1024 * 1024

IN_TILE = 1024
SEQ_TILE = 512
ATT_BLOCK = 256
FF_CHUNK = 512

BF16 = jnp.bfloat16
F32 = jnp.float32
LOG2E = math.log2(math.e)


def _rms(x, g):
    return x * lax.rsqrt(jnp.mean(x * x, axis=-1, keepdims=True) + EPS) * g


def _dot(a, b):
    return jnp.dot(a, b, preferred_element_type=F32)


def _dot_nt(a, b):
    return lax.dot_general(a, b, (((1,), (1,)), ((), ())), preferred_element_type=F32)


def _sigmoid(x):
    return 1.0 / (1.0 + jnp.exp(-x))


def _const_spec(shape):
    n = len(shape)
    return pl.BlockSpec(shape, lambda *_: (0,) * n, pipeline_mode=pl.Buffered(1))


def _params(semantics, flags=None):
    return pltpu.CompilerParams(dimension_semantics=semantics,
                                vmem_limit_bytes=VMEM_LIMIT_BYTES_V7X, flags=flags)


def _row_slices(w, n):
    rows, cols = w.shape
    assert rows % (n * 16) == 0
    return (rows // n, cols)


def _mem_kv_kernel(mem_ref, g_ref, w_ref, win_ref, k_ref, v_ref, winb_ref, wb_ref):
    nb, nm, d = mem_ref.shape

    @pl.when(pl.program_id(0) == 0)
    def _():
        wb_ref[...] = w_ref[...].astype(BF16)

    winb_ref[...] = win_ref[...].astype(BF16)
    m = mem_ref[...].reshape(nb * nm, d)
    mn = _rms(m, g_ref[...]).astype(BF16)
    kv = _dot(mn, wb_ref[...])
    k_ref[...] = kv[:, :d].astype(BF16).reshape(nb, nm, d)
    v_ref[...] = kv[:, d:].astype(BF16).reshape(nb, nm, d)


def _mem_kv(mem, g, w_ckv, w_in):
    b, nm, d = mem.shape
    nb = 2 if b % 2 == 0 else 1
    steps = b // nb
    win_blk = _row_slices(w_in, steps)
    return pl.pallas_call(
        _mem_kv_kernel,
        grid=(steps,),
        in_specs=[pl.BlockSpec((nb, nm, d), lambda i: (i, 0, 0)),
                  _const_spec((1, d)),
                  _const_spec((d, 2 * d)),
                  pl.BlockSpec(win_blk, lambda i: (i, 0))],
        out_specs=[pl.BlockSpec((nb, nm, d), lambda i: (i, 0, 0))] * 2
                  + [pl.BlockSpec(win_blk, lambda i: (i, 0))],
        out_shape=[jax.ShapeDtypeStruct((b, nm, d), BF16)] * 2
                  + [jax.ShapeDtypeStruct(w_in.shape, BF16)],
        scratch_shapes=[pltpu.VMEM((d, 2 * d), BF16)],
        compiler_params=_params(("arbitrary",)),
        name="mem_kv",
    )(mem, g, w_ckv, w_in)


def _in_proj_kernel(x_ref, g_ref, w_ref, cos_ref, sin_ref, qkv_ref, gate_ref, *, q_scale):
    ts = x_ref.shape[1]
    h = _rms(x_ref[0], g_ref[...]).astype(BF16)
    cos = cos_ref[...]
    sin = sin_ref[...]
    lane = lax.broadcasted_iota(jnp.int32, (ts, LANES_V7X), 1)
    first_half = (lane % A_DK) < (A_DK // 2)
    slab_w = LANES_V7X
    grp_w = 4 * slab_w
    gate0 = 6 * grp_w
    for c in range(gate_ref.shape[2] // grp_w):
        z = _dot(h, w_ref[:, gate0 + c * grp_w: gate0 + (c + 1) * grp_w])
        gate_ref[0, :, c * grp_w:(c + 1) * grp_w] = _sigmoid(z).astype(BF16)
    for grp in (0, 1, 3, 2, 4, 5):
        z = _dot(h, w_ref[:, grp * grp_w:(grp + 1) * grp_w])
        for s in range(4):
            zs = z[:, s * slab_w:(s + 1) * slab_w]
            if grp < 2:
                partner = jnp.where(first_half,
                                    pltpu.roll(zs, LANES_V7X - A_DK // 2, 1),
                                    pltpu.roll(zs, A_DK // 2, 1))
                zs = zs * cos + partner * sin
            if grp in (0, 3):
                zs = zs * q_scale
            qkv_ref[0, grp * 4 + s] = zs.astype(BF16)


def _in_proj(x, g, w_in, cos_t, sin_t):
    b, s, d = x.shape
    ts = IN_TILE
    n_gate = w_in.shape[1] - 6 * 4 * LANES_V7X
    assert A_DK == B_DH
    kern = functools.partial(_in_proj_kernel, q_scale=A_DK ** -0.5 * LOG2E)
    return pl.pallas_call(
        kern,
        grid=(b, s // ts),
        in_specs=[pl.BlockSpec((1, ts, d), lambda bi, i: (bi, i, 0)),
                  _const_spec((1, d)),
                  _const_spec(w_in.shape),
                  pl.BlockSpec((ts, LANES_V7X), lambda bi, i: (i, 0)),
                  pl.BlockSpec((ts, LANES_V7X), lambda bi, i: (i, 0))],
        out_specs=[pl.BlockSpec((1, 24, ts, LANES_V7X), lambda bi, i: (bi, 0, i, 0)),
                   pl.BlockSpec((1, ts, n_gate), lambda bi, i: (bi, i, 0))],
        out_shape=[jax.ShapeDtypeStruct((b, 24, s, LANES_V7X), BF16),
                   jax.ShapeDtypeStruct((b, s, n_gate), BF16)],
        compiler_params=_params(("arbitrary", "arbitrary")),
        name="in_proj",
    )(x, g, w_in, cos_t, sin_t)


def _stack_halves(q, split):
    lane = lax.broadcasted_iota(jnp.int32, q.shape, 1)
    zero = jnp.zeros_like(q)
    return jnp.concatenate([jnp.where(lane < split, q, zero),
                            jnp.where(lane >= split, q, zero)], axis=0)


ONES_ROWS = 16


def _load_vt(v_ref, vt_ref):
    w, seq = v_ref.shape[3], v_ref.shape[2]
    vt_ref[:w, :] = v_ref[0, 0].T
    vt_ref[w:, :] = jnp.ones((ONES_ROWS, seq), BF16)


def _row_softmax_pv(s, v):
    p = jnp.exp(s - jnp.max(s, axis=1, keepdims=True))
    return _dot(p.astype(BF16), v), jnp.sum(p, axis=1, keepdims=True)


def _diff_attn_kernel(q_ref, k_ref, v_ref, lq1_ref, lk1_ref, lq2_ref, lk2_ref, g_ref,
                      o_ref, vt_ref, s0_ref, s1_ref, p0_ref, p1_ref, *, lam_init):
    tq = ATT_BLOCK
    seq, w = q_ref.shape[2], q_ref.shape[3]
    _load_vt(v_ref, vt_ref)
    lam = (jnp.exp(jnp.sum(lq1_ref[...] * lk1_ref[...], axis=1, keepdims=True))
           - jnp.exp(jnp.sum(lq2_ref[...] * lk2_ref[...], axis=1, keepdims=True))
           + lam_init)
    key = lax.broadcasted_iota(jnp.int32, (tq, 2 * tq), 0)
    qry = lax.broadcasted_iota(jnp.int32, (tq, 2 * tq), 1)
    allowed = (key // CHUNK) <= ((qry % tq) // CHUNK)

    s_refs, p_refs = (s0_ref, s1_ref), (p0_ref, p1_ref)

    def scores(slot, qi):
        qq = _stack_halves(q_ref[0, 0, qi * tq:(qi + 1) * tq, :], A_DK)
        m8 = None
        for kj in range(qi + 1):
            rows = slice(kj * tq, (kj + 1) * tq)
            s = _dot_nt(k_ref[0, 0, rows, :], qq)
            if kj == qi:
                s = jnp.where(allowed, s, NEG)
            s_refs[slot][rows, :] = s
            mj = jnp.max(s.reshape(tq // 8, 8, 2 * tq), axis=0)
            m8 = mj if m8 is None else jnp.maximum(m8, mj)
        return jnp.max(m8, axis=0, keepdims=True)

    def probs(slot, qi, m):
        for kj in range(qi + 1):
            rows = slice(kj * tq, (kj + 1) * tq)
            p = jnp.exp2(s_refs[slot][rows, :] - m)
            p_refs[slot][rows, :] = p.astype(BF16)

    def values(slot, qi):
        q0, q1 = qi * tq, (qi + 1) * tq
        ot = _dot(vt_ref[:, :q1], p_refs[slot][:q1, :])
        ot = ot[:w] / ot[w:w + 1]
        o = (ot[:, :tq] - lam * ot[:, tq:]).T
        o = _rms(o, g_ref[...]) * (1.0 - lam_init)
        o_ref[0, q0:q1, :] = o.astype(BF16)

    nq = seq // tq
    order = list(range(nq))
    m_cur = scores(0, order[0])
    for pos, qi in enumerate(order):
        m_next = scores((pos + 1) % 2, order[pos + 1]) if pos + 1 < nq else None
        if pos:
            values((pos - 1) % 2, order[pos - 1])
        probs(pos % 2, qi, m_cur)
        m_cur = m_next
    values((nq - 1) % 2, order[-1])


def _band_tiles_init(bias_ref, tab_ref):
    t = ATT_BLOCK
    kc = lax.broadcasted_iota(jnp.int32, (t, t), 0) // CHUNK
    qc = lax.broadcasted_iota(jnp.int32, (t, t), 1) // CHUNK
    for head in range(2):
        cols = slice(head * t, (head + 1) * t)
        for back in range(2):
            line = jnp.broadcast_to(bias_ref[0, head, back:back + 1, :], (t, 2 * t))
            tile = pltpu.roll(line, 0, 1, stride=1, stride_axis=0)[:, :t]
            tab_ref[back, :, cols] = jnp.where(kc <= qc, tile, NEG) if back == 0 else tile
        far = jnp.broadcast_to(bias_ref[0, head, 1:2, 0:1], (t, t))
        tab_ref[2, :, cols] = jnp.where(kc >= qc, far, NEG)


def _band_attn_kernel(q_ref, k_ref, v_ref, bias_ref, o_ref, tab_ref, vt_ref,
                      s0_ref, s1_ref, p0_ref, p1_ref):
    tq = ATT_BLOCK
    seq, w = q_ref.shape[2], q_ref.shape[3]
    _load_vt(v_ref, vt_ref)
    first_head = lax.broadcasted_iota(jnp.int32, (LANES_V7X, tq), 0) < B_DH
    s_refs, p_refs = (s0_ref, s1_ref), (p0_ref, p1_ref)

    def scores(qi):
        qq = _stack_halves(q_ref[0, 0, qi * tq:(qi + 1) * tq, :], B_DH)
        m8 = None
        for back in range(min(qi, 2) + 1):
            kj = qi - back
            s = _dot_nt(k_ref[0, 0, kj * tq:(kj + 1) * tq, :], qq) + tab_ref[back]
            s_refs[qi % 2][back * tq:(back + 1) * tq, :] = s
            mj = jnp.max(s.reshape(tq // 8, 8, 2 * tq), axis=0)
            m8 = mj if m8 is None else jnp.maximum(m8, mj)
        return jnp.max(m8, axis=0, keepdims=True)

    def probs(qi, m):
        nblk = min(qi, 2) + 1
        for back in range(nblk):
            p = jnp.exp2(s_refs[qi % 2][back * tq:(back + 1) * tq, :] - m)
            pos = nblk - 1 - back
            p_refs[qi % 2][pos * tq:(pos + 1) * tq, :] = p.astype(BF16)

    def values(qi):
        q0, q1 = qi * tq, (qi + 1) * tq
        nblk = min(qi, 2) + 1
        ot = _dot(vt_ref[:, q1 - nblk * tq:q1], p_refs[qi % 2][:nblk * tq, :])
        ot = ot[:w] / ot[w:w + 1]
        o = jnp.where(first_head, ot[:, :tq], ot[:, tq:]).T
        o_ref[0, q0:q1, :] = o.astype(BF16)

    nq = seq // tq
    m_cur = scores(0)
    for qi in range(nq):
        m_next = scores(qi + 1) if qi + 1 < nq else None
        probs(qi, m_cur)
        values(qi)
        m_cur = m_next


def _band_bias_lines(rel_bias):
    t = ATT_BLOCK
    u = jnp.arange(2 * t)
    u = jnp.where(u < t, u, u - 2 * t)
    idx = jnp.stack([jnp.clip(u, -B_MAX_REL, B_MAX_REL), jnp.clip(t + u, -B_MAX_REL, B_MAX_REL)])
    lines = rel_bias[:, idx + B_MAX_REL].astype(F32) * LOG2E
    return lines.reshape(B_HEADS // 2, 2, 2, 2 * t)


N_SIDE_CASTS = 7


def _attn_kernel(*refs, lam_init):
    (qa_ref, ka_ref, va_ref, lq1_ref, lk1_ref, lq2_ref, lk2_ref, g_ref,
     qb_ref, kb_ref, vb_ref, bias_ref) = refs[:12]
    w_f32 = refs[12:12 + N_SIDE_CASTS]
    ya_ref, yb_ref = refs[12 + N_SIDE_CASTS:14 + N_SIDE_CASTS]
    w_bf16 = refs[14 + N_SIDE_CASTS:14 + 2 * N_SIDE_CASTS]
    (vta_ref, sa0_ref, sa1_ref, pa0_ref, pa1_ref,
     tab_ref, vtb_ref, sb0_ref, sb1_ref, pb0_ref, pb1_ref) = refs[14 + 2 * N_SIDE_CASTS:]
    pair_tab = tab_ref.at[pl.program_id(1)]

    @pl.when(pl.program_id(0) == 0)
    def _():
        _band_tiles_init(bias_ref, pair_tab)

    @pl.when(pl.program_id(1) == 0)
    def _():
        for src, dst in zip(w_f32, w_bf16):
            dst[...] = src[...].astype(BF16)

    _diff_attn_kernel(qa_ref, ka_ref, va_ref, lq1_ref, lk1_ref, lq2_ref, lk2_ref, g_ref,
                      ya_ref, vta_ref, sa0_ref, sa1_ref, pa0_ref, pa1_ref, lam_init=lam_init)
    _band_attn_kernel(qb_ref, kb_ref, vb_ref, bias_ref, yb_ref, pair_tab, vtb_ref,
                      sb0_ref, sb1_ref, pb0_ref, pb1_ref)


def _attn(qkv, lq1, lk1, lq2, lk2, subln_g, lines, weights, lam_init):
    b, _, s, w = qkv.shape
    tq = ATT_BLOCK
    npair = B_HEADS // 2
    assert npair == A_HEADS and len(weights) == N_SIDE_CASTS
    kern = functools.partial(_attn_kernel, lam_init=lam_init)
    vec = _const_spec((1, A_DK))
    slab = lambda g0: pl.BlockSpec((1, 1, s, w), lambda bi, h: (bi, g0 + h, 0, 0))
    out = pl.BlockSpec((1, s, w), lambda bi, h: (bi, 0, h))
    w_specs = [pl.BlockSpec(_row_slices(wt, b), lambda bi, h: (bi, 0)) for wt in weights]
    return pl.pallas_call(
        kern,
        grid=(b, A_HEADS),
        in_specs=[slab(0), slab(4), slab(8), vec, vec, vec, vec, _const_spec((1, A_DV)),
                  slab(12), slab(16), slab(20),
                  pl.BlockSpec((1, 2, 2, 2 * tq), lambda bi, h: (h, 0, 0, 0))] + w_specs,
        out_specs=[out, out] + w_specs,
        out_shape=[jax.ShapeDtypeStruct((b, s, A_HEADS * A_DV), BF16),
                   jax.ShapeDtypeStruct((b, s, B_HEADS * B_DH), BF16)]
                  + [jax.ShapeDtypeStruct(wt.shape, BF16) for wt in weights],
        scratch_shapes=[pltpu.VMEM((w + ONES_ROWS, s), BF16),
                        pltpu.VMEM((s, 2 * tq), F32), pltpu.VMEM((s, 2 * tq), F32),
                        pltpu.VMEM((s, 2 * tq), BF16), pltpu.VMEM((s, 2 * tq), BF16),
                        pltpu.VMEM((npair, 3, tq, 2 * tq), F32),
                        pltpu.VMEM((w + ONES_ROWS, s), BF16),
                        pltpu.VMEM((3 * tq, 2 * tq), F32), pltpu.VMEM((3 * tq, 2 * tq), F32),
                        pltpu.VMEM((3 * tq, 2 * tq), BF16), pltpu.VMEM((3 * tq, 2 * tq), BF16)],
        compiler_params=_params(("arbitrary", "arbitrary")),
        name="attn",
    )(qkv, qkv, qkv, lq1, lk1, lq2, lk2, subln_g, qkv, qkv, qkv, lines, *weights)


def _post_mix_kernel(x_ref, ya_ref, yb_ref, gate_ref, km_ref, vm_ref,
                     wua_ref, wub_ref, wo_ref, gc_ref, wcq_ref, wco_ref,
                     gf_ref, wgu_ref, wd_ref, gfin_ref, o_ref, *, final_norm):
    d = x_ref.shape[2]
    ua = _dot(ya_ref[0], wua_ref[...])
    ub = _dot(yb_ref[0], wub_ref[...])
    merged = (gate_ref[0, :, :d].astype(F32) * ua + gate_ref[0, :, d:].astype(F32) * ub)
    x = x_ref[0] + _dot(merged.astype(BF16), wo_ref[...])

    dh = d // X_HEADS
    hc = _rms(x, gc_ref[...]).astype(BF16)
    q = (_dot(hc, wcq_ref[...]) * (dh ** -0.5)).astype(BF16)
    heads = []
    for h in range(X_HEADS):
        sl = slice(h * dh, (h + 1) * dh)
        pv, l = _row_softmax_pv(_dot_nt(q[:, sl], km_ref[0, :, sl]), vm_ref[0, :, sl])
        heads.append((pv / l).astype(BF16))
    x = x + _dot(jnp.concatenate(heads, axis=1), wco_ref[...])

    hf = _rms(x, gf_ref[...]).astype(BF16)
    dff = wd_ref.shape[0]
    acc = jnp.zeros_like(x)
    for c0 in range(0, dff, FF_CHUNK):
        c1 = min(c0 + FF_CHUNK, dff)
        g = _dot(hf, wgu_ref[:, c0:c1])
        u = _dot(hf, wgu_ref[:, dff + c0:dff + c1])
        a = (g * _sigmoid(g) * u).astype(BF16)
        acc = acc + _dot(a, wd_ref[c0:c1, :])
    x = x + acc
    if final_norm:
        x = _rms(x, gfin_ref[...])
    o_ref[0] = x


def _post_mix(x, ya, yb, gates, km, vm, wua, wub, wo, gc, wcq, wco, gf, wgu, wd, gfin,
              final_norm):
    b, s, d = x.shape
    ts = SEQ_TILE
    nm = km.shape[1]
    row = lambda w: pl.BlockSpec((1, ts, w), lambda bi, i: (bi, i, 0))
    mem = pl.BlockSpec((1, nm, d), lambda bi, i: (bi, 0, 0))
    kern = functools.partial(_post_mix_kernel, final_norm=final_norm)
    return pl.pallas_call(
        kern,
        grid=(b, s // ts),
        in_specs=[row(d), row(ya.shape[2]), row(yb.shape[2]), row(gates.shape[2]), mem, mem,
                  _const_spec(wua.shape), _const_spec(wub.shape), _const_spec(wo.shape),
                  _const_spec((1, d)), _const_spec(wcq.shape), _const_spec(wco.shape),
                  _const_spec((1, d)), _const_spec(wgu.shape), _const_spec(wd.shape),
                  _const_spec((1, d))],
        out_specs=row(d),
        out_shape=jax.ShapeDtypeStruct((b, s, d), F32),
        compiler_params=_params(("arbitrary", "arbitrary")),
        name="post_mix",
    )(x, ya, yb, gates, km, vm, wua, wub, wo, gc, wcq, wco, gf, wgu, wd, gfin)


def _rope_tables(seq):
    inv = 1.0 / (ROPE_THETA ** (jnp.arange(0, A_DK, 2, dtype=F32) / A_DK))
    ang = jnp.arange(seq, dtype=F32)[:, None] * inv[None, :]
    cos, sin = jnp.cos(ang), jnp.sin(ang)
    reps = LANES_V7X // A_DK
    cos_t = jnp.tile(jnp.concatenate([cos, cos], axis=1), (1, reps))
    sin_t = jnp.tile(jnp.concatenate([-sin, sin], axis=1), (1, reps))
    return cos_t, sin_t


def kernel(x, mem, norm_mix_g, w_in, lam_q1, lam_k1, lam_q2, lam_k2, subln_g, rel_bias, w_up_a, w_up_b, w_out, norm_cross_g, norm_mem_g, w_cq, w_ckv, w_co, norm_ffn_g, w_gate_up, w_down, norm_final_g):
    b, s, d = x.shape
    depth = w_in.shape[0]
    assert s % SEQ_TILE == 0 and s % IN_TILE == 0 and s % ATT_BLOCK == 0 and ATT_BLOCK == 4 * CHUNK
    assert B_LEFT_CHUNKS * CHUNK == 2 * ATT_BLOCK and B_MAX_REL <= ATT_BLOCK
    cos_t, sin_t = _rope_tables(s)
    row = lambda v: v.reshape(1, -1).astype(F32)
    for layer in range(depth):
        lam_init = 0.8 - 0.6 * math.exp(-0.3 * layer)
        km, vm, w_in_b = _mem_kv(mem, row(norm_mem_g[layer]), w_ckv[layer], w_in[layer])
        qkv, gates = _in_proj(x, row(norm_mix_g[layer]), w_in_b, cos_t, sin_t)
        ya, yb, wua, wub, wo, wcq, wco, wgu, wd = _attn(
            qkv, row(lam_q1[layer]), row(lam_k1[layer]), row(lam_q2[layer]),
            row(lam_k2[layer]), row(subln_g[layer]), _band_bias_lines(rel_bias[layer]),
            [w_up_a[layer], w_up_b[layer], w_out[layer], w_cq[layer], w_co[layer],
             w_gate_up[layer], w_down[layer]], lam_init)
        x = _post_mix(x, ya, yb, gates, km, vm, wua, wub, wo, row(norm_cross_g[layer]),
                      wcq, wco, row(norm_ffn_g[layer]), wgu, wd, row(norm_final_g),
                      final_norm=(layer == depth - 1))
    return x
```

```python
import functools
import math

import jax
import jax.numpy as jnp
from jax import lax
from jax.experimental import pallas as pl
from jax.experimental.pallas import tpu as pltpu

CHUNK = 64
EPS = 1e-6
ROPE_THETA = 10000.0
NEG = -1e30

A_HEADS = 4
A_DK = 64
A_DV = 2 * A_DK
B_HEADS = 8
B_DH = 64
B_LEFT_CHUNKS = 8
B_MAX_REL = 256
X_HEADS = 4

LANES_V7X = 128
VMEM_LIMIT_BYTES_V7X = 56 * 1024 * 1024

IN_TILE = 1024
SEQ_TILE = 512
ATT_BLOCK = 256
FF_CHUNK = 512

BF16 = jnp.bfloat16
F32 = jnp.float32
LOG2E = math.log2(math.e)


def _rms(x, g):
    return x * lax.rsqrt(jnp.mean(x * x, axis=-1, keepdims=True) + EPS) * g


def _dot(a, b):
    return jnp.dot(a, b, preferred_element_type=F32)


def _dot_nt(a, b):
    return lax.dot_general(a, b, (((1,), (1,)), ((), ())), preferred_element_type=F32)


def _sigmoid(x):
    return 1.0 / (1.0 + jnp.exp(-x))


def _const_spec(shape):
    n = len(shape)
    return pl.BlockSpec(shape, lambda *_: (0,) * n, pipeline_mode=pl.Buffered(1))


def _params(semantics, flags=None):
    return pltpu.CompilerParams(dimension_semantics=semantics,
                                vmem_limit_bytes=VMEM_LIMIT_BYTES_V7X, flags=flags)


def _row_slices(w, n):
    rows, cols = w.shape
    assert rows % (n * 16) == 0
    return (rows // n, cols)


def _mem_kv_kernel(mem_ref, g_ref, w_ref, win_ref, k_ref, v_ref, winb_ref, wb_ref):
    nb, nm, d = mem_ref.shape

    @pl.when(pl.program_id(0) == 0)
    def _():
        wb_ref[...] = w_ref[...].astype(BF16)

    winb_ref[...] = win_ref[...].astype(BF16)
    m = mem_ref[...].reshape(nb * nm, d)
    mn = _rms(m, g_ref[...]).astype(BF16)
    kv = _dot(mn, wb_ref[...])
    k_ref[...] = kv[:, :d].astype(BF16).reshape(nb, nm, d)
    v_ref[...] = kv[:, d:].astype(BF16).reshape(nb, nm, d)


def _mem_kv(mem, g, w_ckv, w_in):
    b, nm, d = mem.shape
    nb = 2 if b % 2 == 0 else 1
    steps = b // nb
    win_blk = _row_slices(w_in, steps)
    return pl.pallas_call(
        _mem_kv_kernel,
        grid=(steps,),
        in_specs=[pl.BlockSpec((nb, nm, d), lambda i: (i, 0, 0)),
                  _const_spec((1, d)),
                  _const_spec((d, 2 * d)),
                  pl.BlockSpec(win_blk, lambda i: (i, 0))],
        out_specs=[pl.BlockSpec((nb, nm, d), lambda i: (i, 0, 0))] * 2
                  + [pl.BlockSpec(win_blk, lambda i: (i, 0))],
        out_shape=[jax.ShapeDtypeStruct((b, nm, d), BF16)] * 2
                  + [jax.ShapeDtypeStruct(w_in.shape, BF16)],
        scratch_shapes=[pltpu.VMEM((d, 2 * d), BF16)],
        compiler_params=_params(("arbitrary",)),
        name="mem_kv",
    )(mem, g, w_ckv, w_in)


N_SIDE_CASTS = 7


def _in_proj_kernel(*refs, q_scale):
    x_ref, g_ref, w_ref, cos_ref, sin_ref = refs[:5]
    w_f32 = refs[5:5 + N_SIDE_CASTS]
    qkv_ref, gate_ref = refs[5 + N_SIDE_CASTS:7 + N_SIDE_CASTS]
    w_bf16 = refs[7 + N_SIDE_CASTS:]

    @pl.when(pl.program_id(1) == 0)
    def _():
        for src, dst in zip(w_f32, w_bf16):
            dst[...] = src[...].astype(BF16)

    ts = x_ref.shape[1]
    h = _rms(x_ref[0], g_ref[...]).astype(BF16)
    cos = cos_ref[...]
    sin = sin_ref[...]
    lane = lax.broadcasted_iota(jnp.int32, (ts, LANES_V7X), 1)
    first_half = (lane % A_DK) < (A_DK // 2)
    slab_w = LANES_V7X
    grp_w = 4 * slab_w
    gate0 = 6 * grp_w
    for c in range(gate_ref.shape[2] // grp_w):
        z = _dot(h, w_ref[:, gate0 + c * grp_w: gate0 + (c + 1) * grp_w])
        gate_ref[0, :, c * grp_w:(c + 1) * grp_w] = _sigmoid(z).astype(BF16)
    for grp in (0, 1, 3, 2, 4, 5):
        z = _dot(h, w_ref[:, grp * grp_w:(grp + 1) * grp_w])
        for s in range(4):
            zs = z[:, s * slab_w:(s + 1) * slab_w]
            if grp < 2:
                partner = jnp.where(first_half,
                                    pltpu.roll(zs, LANES_V7X - A_DK // 2, 1),
                                    pltpu.roll(zs, A_DK // 2, 1))
                zs = zs * cos + partner * sin
            if grp in (0, 3):
                zs = zs * q_scale
            qkv_ref[0, grp * 4 + s] = zs.astype(BF16)


def _in_proj(x, g, w_in, cos_t, sin_t, weights):
    b, s, d = x.shape
    ts = IN_TILE
    n_gate = w_in.shape[1] - 6 * 4 * LANES_V7X
    assert A_DK == B_DH and len(weights) == N_SIDE_CASTS
    kern = functools.partial(_in_proj_kernel, q_scale=A_DK ** -0.5 * LOG2E)
    w_specs = [pl.BlockSpec(_row_slices(wt, b), lambda bi, i: (bi, 0)) for wt in weights]
    return pl.pallas_call(
        kern,
        grid=(b, s // ts),
        in_specs=[pl.BlockSpec((1, ts, d), lambda bi, i: (bi, i, 0)),
                  _const_spec((1, d)),
                  _const_spec(w_in.shape),
                  pl.BlockSpec((ts, LANES_V7X), lambda bi, i: (i, 0)),
                  pl.BlockSpec((ts, LANES_V7X), lambda bi, i: (i, 0))] + w_specs,
        out_specs=[pl.BlockSpec((1, 24, ts, LANES_V7X), lambda bi, i: (bi, 0, i, 0)),
                   pl.BlockSpec((1, ts, n_gate), lambda bi, i: (bi, i, 0))] + w_specs,
        out_shape=[jax.ShapeDtypeStruct((b, 24, s, LANES_V7X), BF16),
                   jax.ShapeDtypeStruct((b, s, n_gate), BF16)]
                  + [jax.ShapeDtypeStruct(wt.shape, BF16) for wt in weights],
        compiler_params=_params(("arbitrary", "arbitrary")),
        name="in_proj",
    )(x, g, w_in, cos_t, sin_t, *weights)


def _stack_halves(q, split):
    lane = lax.broadcasted_iota(jnp.int32, q.shape, 1)
    zero = jnp.zeros_like(q)
    return jnp.concatenate([jnp.where(lane < split, q, zero),
                            jnp.where(lane >= split, q, zero)], axis=0)


ONES_ROWS = 16


def _load_vt(v_ref, vt_ref):
    w, seq = v_ref.shape[3], v_ref.shape[2]
    vt_ref[:w, :] = v_ref[0, 0].T
    vt_ref[w:, :] = jnp.ones((ONES_ROWS, seq), BF16)


def _row_softmax_pv(s, v):
    p = jnp.exp(s - jnp.max(s, axis=1, keepdims=True))
    return _dot(p.astype(BF16), v), jnp.sum(p, axis=1, keepdims=True)


def _diff_attn_kernel(q_ref, k_ref, v_ref, lq1_ref, lk1_ref, lq2_ref, lk2_ref, g_ref,
                      o_ref, vt_ref, s0_ref, s1_ref, p0_ref, p1_ref, *, lam_init):
    tq = ATT_BLOCK
    seq, w = q_ref.shape[2], q_ref.shape[3]
    _load_vt(v_ref, vt_ref)
    lam = (jnp.exp(jnp.sum(lq1_ref[...] * lk1_ref[...], axis=1, keepdims=True))
           - jnp.exp(jnp.sum(lq2_ref[...] * lk2_ref[...], axis=1, keepdims=True))
           + lam_init)
    key = lax.broadcasted_iota(jnp.int32, (tq, 2 * tq), 0)
    qry = lax.broadcasted_iota(jnp.int32, (tq, 2 * tq), 1)
    allowed = (key // CHUNK) <= ((qry % tq) // CHUNK)

    s_refs, p_refs = (s0_ref, s1_ref), (p0_ref, p1_ref)

    def scores(slot, qi):
        qq = _stack_halves(q_ref[0, 0, qi * tq:(qi + 1) * tq, :], A_DK)
        m8 = None
        for kj in range(qi + 1):
            rows = slice(kj * tq, (kj + 1) * tq)
            s = _dot_nt(k_ref[0, 0, rows, :], qq)
            if kj == qi:
                s = jnp.where(allowed, s, NEG)
            s_refs[slot][rows, :] = s
            mj = jnp.max(s.reshape(tq // 8, 8, 2 * tq), axis=0)
            m8 = mj if m8 is None else jnp.maximum(m8, mj)
        return jnp.max(m8, axis=0, keepdims=True)

    def probs(slot, qi, m):
        for kj in range(qi + 1):
            rows = slice(kj * tq, (kj + 1) * tq)
            p = jnp.exp2(s_refs[slot][rows, :] - m)
            p_refs[slot][rows, :] = p.astype(BF16)

    def values(slot, qi):
        q0, q1 = qi * tq, (qi + 1) * tq
        ot = _dot(vt_ref[:, :q1], p_refs[slot][:q1, :])
        ot = ot[:w] / ot[w:w + 1]
        o = (ot[:, :tq] - lam * ot[:, tq:]).T
        o = _rms(o, g_ref[...]) * (1.0 - lam_init)
        o_ref[0, q0:q1, :] = o.astype(BF16)

    nq = seq // tq
    order = list(range(nq))
    m_cur = scores(0, order[0])
    for pos, qi in enumerate(order):
        m_next = scores((pos + 1) % 2, order[pos + 1]) if pos + 1 < nq else None
        if pos:
            values((pos - 1) % 2, order[pos - 1])
        probs(pos % 2, qi, m_cur)
        m_cur = m_next
    values((nq - 1) % 2, order[-1])


def _band_tiles_init(bias_ref, tab_ref):
    t = ATT_BLOCK
    kc = lax.broadcasted_iota(jnp.int32, (t, t), 0) // CHUNK
    qc = lax.broadcasted_iota(jnp.int32, (t, t), 1) // CHUNK
    for head in range(2):
        cols = slice(head * t, (head + 1) * t)
        for back in range(2):
            line = jnp.broadcast_to(bias_ref[0, head, back:back + 1, :], (t, 2 * t))
            tile = pltpu.roll(line, 0, 1, stride=1, stride_axis=0)[:, :t]
            tab_ref[back, :, cols] = jnp.where(kc <= qc, tile, NEG) if back == 0 else tile
        far = jnp.broadcast_to(bias_ref[0, head, 1:2, 0:1], (t, t))
        tab_ref[2, :, cols] = jnp.where(kc >= qc, far, NEG)


def _band_attn_kernel(q_ref, k_ref, v_ref, bias_ref, o_ref, tab_ref, vt_ref,
                      s0_ref, s1_ref, p0_ref, p1_ref):
    tq = ATT_BLOCK
    seq, w = q_ref.shape[2], q_ref.shape[3]
    _load_vt(v_ref, vt_ref)
    first_head = lax.broadcasted_iota(jnp.int32, (LANES_V7X, tq), 0) < B_DH
    s_refs, p_refs = (s0_ref, s1_ref), (p0_ref, p1_ref)

    def scores(qi):
        qq = _stack_halves(q_ref[0, 0, qi * tq:(qi + 1) * tq, :], B_DH)
        m8 = None
        for back in range(min(qi, 2) + 1):
            kj = qi - back
            s = _dot_nt(k_ref[0, 0, kj * tq:(kj + 1) * tq, :], qq) + tab_ref[back]
            s_refs[qi % 2][back * tq:(back + 1) * tq, :] = s
            mj = jnp.max(s.reshape(tq // 8, 8, 2 * tq), axis=0)
            m8 = mj if m8 is None else jnp.maximum(m8, mj)
        return jnp.max(m8, axis=0, keepdims=True)

    def probs(qi, m):
        nblk = min(qi, 2) + 1
        for back in range(nblk):
            p = jnp.exp2(s_refs[qi % 2][back * tq:(back + 1) * tq, :] - m)
            pos = nblk - 1 - back
            p_refs[qi % 2][pos * tq:(pos + 1) * tq, :] = p.astype(BF16)

    def values(qi):
        q0, q1 = qi * tq, (qi + 1) * tq
        nblk = min(qi, 2) + 1
        ot = _dot(vt_ref[:, q1 - nblk * tq:q1], p_refs[qi % 2][:nblk * tq, :])
        ot = ot[:w] / ot[w:w + 1]
        o = jnp.where(first_head, ot[:, :tq], ot[:, tq:]).T
        o_ref[0, q0:q1, :] = o.astype(BF16)

    nq = seq // tq
    m_cur = scores(0)
    for qi in range(nq):
        m_next = scores(qi + 1) if qi + 1 < nq else None
        probs(qi, m_cur)
        values(qi)
        m_cur = m_next


def _band_bias_lines(rel_bias):
    t = ATT_BLOCK
    u = jnp.arange(2 * t)
    u = jnp.where(u < t, u, u - 2 * t)
    idx = jnp.stack([jnp.clip(u, -B_MAX_REL, B_MAX_REL), jnp.clip(t + u, -B_MAX_REL, B_MAX_REL)])
    lines = rel_bias[:, idx + B_MAX_REL].astype(F32) * LOG2E
    return lines.reshape(B_HEADS // 2, 2, 2, 2 * t)


def _attn_kernel(qa_ref, ka_ref, va_ref, lq1_ref, lk1_ref, lq2_ref, lk2_ref, g_ref,
                 qb_ref, kb_ref, vb_ref, bias_ref, ya_ref, yb_ref,
                 vta_ref, sa0_ref, sa1_ref, pa0_ref, pa1_ref,
                 tab_ref, vtb_ref, sb0_ref, sb1_ref, pb0_ref, pb1_ref, *, lam_init):
    pair_tab = tab_ref.at[pl.program_id(1)]

    @pl.when(pl.program_id(0) == 0)
    def _():
        _band_tiles_init(bias_ref, pair_tab)

    _diff_attn_kernel(qa_ref, ka_ref, va_ref, lq1_ref, lk1_ref, lq2_ref, lk2_ref, g_ref,
                      ya_ref, vta_ref, sa0_ref, sa1_ref, pa0_ref, pa1_ref, lam_init=lam_init)
    _band_attn_kernel(qb_ref, kb_ref, vb_ref, bias_ref, yb_ref, pair_tab, vtb_ref,
                      sb0_ref, sb1_ref, pb0_ref, pb1_ref)


def _attn(qkv, lq1, lk1, lq2, lk2, subln_g, lines, lam_init):
    b, _, s, w = qkv.shape
    tq = ATT_BLOCK
    npair = B_HEADS // 2
    assert npair == A_HEADS
    kern = functools.partial(_attn_kernel, lam_init=lam_init)
    vec = _const_spec((1, A_DK))
    slab = lambda g0: pl.BlockSpec((1, 1, s, w), lambda bi, h: (bi, g0 + h, 0, 0))
    out = pl.BlockSpec((1, s, w), lambda bi, h: (bi, 0, h))
    return pl.pallas_call(
        kern,
        grid=(b, A_HEADS),
        in_specs=[slab(0), slab(4), slab(8), vec, vec, vec, vec, _const_spec((1, A_DV)),
                  slab(12), slab(16), slab(20),
                  pl.BlockSpec((1, 2, 2, 2 * tq), lambda bi, h: (h, 0, 0, 0))],
        out_specs=[out, out],
        out_shape=[jax.ShapeDtypeStruct((b, s, A_HEADS * A_DV), BF16),
                   jax.ShapeDtypeStruct((b, s, B_HEADS * B_DH), BF16)],
        scratch_shapes=[pltpu.VMEM((w + ONES_ROWS, s), BF16),
                        pltpu.VMEM((s, 2 * tq), F32), pltpu.VMEM((s, 2 * tq), F32),
                        pltpu.VMEM((s, 2 * tq), BF16), pltpu.VMEM((s, 2 * tq), BF16),
                        pltpu.VMEM((npair, 3, tq, 2 * tq), F32),
                        pltpu.VMEM((w + ONES_ROWS, s), BF16),
                        pltpu.VMEM((3 * tq, 2 * tq), F32), pltpu.VMEM((3 * tq, 2 * tq), F32),
                        pltpu.VMEM((3 * tq, 2 * tq), BF16), pltpu.VMEM((3 * tq, 2 * tq), BF16)],
        compiler_params=_params(("arbitrary", "arbitrary")),
        name="attn",
    )(qkv, qkv, qkv, lq1, lk1, lq2, lk2, subln_g, qkv, qkv, qkv, lines)


def _post_mix_kernel(x_ref, ya_ref, yb_ref, gate_ref, km_ref, vm_ref,
                     wua_ref, wub_ref, wo_ref, gc_ref, wcq_ref, wco_ref,
                     gf_ref, wgu_ref, wd_ref, gfin_ref, o_ref, *, final_norm):
    d = x_ref.shape[2]
    ua = _dot(ya_ref[0], wua_ref[...])
    ub = _dot(yb_ref[0], wub_ref[...])
    merged = (gate_ref[0, :, :d].astype(F32) * ua + gate_ref[0, :, d:].astype(F32) * ub)
    x = x_ref[0] + _dot(merged.astype(BF16), wo_ref[...])

    dh = d // X_HEADS
    hc = _rms(x, gc_ref[...]).astype(BF16)
    q = (_dot(hc, wcq_ref[...]) * (dh ** -0.5)).astype(BF16)
    heads = []
    for h in range(X_HEADS):
        sl = slice(h * dh, (h + 1) * dh)
        pv, l = _row_softmax_pv(_dot_nt(q[:, sl], km_ref[0, :, sl]), vm_ref[0, :, sl])
        heads.append((pv / l).astype(BF16))
    x = x + _dot(jnp.concatenate(heads, axis=1), wco_ref[...])

    hf = _rms(x, gf_ref[...]).astype(BF16)
    dff = wd_ref.shape[0]
    acc = jnp.zeros_like(x)
    for c0 in range(0, dff, FF_CHUNK):
        c1 = min(c0 + FF_CHUNK, dff)
        g = _dot(hf, wgu_ref[:, c0:c1])
        u = _dot(hf, wgu_ref[:, dff + c0:dff + c1])
        a = (g * _sigmoid(g) * u).astype(BF16)
        acc = acc + _dot(a, wd_ref[c0:c1, :])
    x = x + acc
    if final_norm:
        x = _rms(x, gfin_ref[...])
    o_ref[0] = x


def _post_mix(x, ya, yb, gates, km, vm, wua, wub, wo, gc, wcq, wco, gf, wgu, wd, gfin,
              final_norm):
    b, s, d = x.shape
    ts = SEQ_TILE
    nm = km.shape[1]
    row = lambda w: pl.BlockSpec((1, ts, w), lambda bi, i: (bi, i, 0))
    mem = pl.BlockSpec((1, nm, d), lambda bi, i: (bi, 0, 0))
    kern = functools.partial(_post_mix_kernel, final_norm=final_norm)
    return pl.pallas_call(
        kern,
        grid=(b, s // ts),
        in_specs=[row(d), row(ya.shape[2]), row(yb.shape[2]), row(gates.shape[2]), mem, mem,
                  _const_spec(wua.shape), _const_spec(wub.shape), _const_spec(wo.shape),
                  _const_spec((1, d)), _const_spec(wcq.shape), _const_spec(wco.shape),
                  _const_spec((1, d)), _const_spec(wgu.shape), _const_spec(wd.shape),
                  _const_spec((1, d))],
        out_specs=row(d),
        out_shape=jax.ShapeDtypeStruct((b, s, d), F32),
        compiler_params=_params(("arbitrary", "arbitrary")),
        name="post_mix",
    )(x, ya, yb, gates, km, vm, wua, wub, wo, gc, wcq, wco, gf, wgu, wd, gfin)


def _rope_tables(seq):
    inv = 1.0 / (ROPE_THETA ** (jnp.arange(0, A_DK, 2, dtype=F32) / A_DK))
    ang = jnp.arange(seq, dtype=F32)[:, None] * inv[None, :]
    cos, sin = jnp.cos(ang), jnp.sin(ang)
    reps = LANES_V7X // A_DK
    cos_t = jnp.tile(jnp.concatenate([cos, cos], axis=1), (1, reps))
    sin_t = jnp.tile(jnp.concatenate([-sin, sin], axis=1), (1, reps))
    return cos_t, sin_t


def kernel(x, mem, norm_mix_g, w_in, lam_q1, lam_k1, lam_q2, lam_k2, subln_g, rel_bias, w_up_a, w_up_b, w_out, norm_cross_g, norm_mem_g, w_cq, w_ckv, w_co, norm_ffn_g, w_gate_up, w_down, norm_final_g):
    b, s, d = x.shape
    depth = w_in.shape[0]
    assert s % SEQ_TILE == 0 and s % IN_TILE == 0 and s % ATT_BLOCK == 0 and ATT_BLOCK == 4 * CHUNK
    assert B_LEFT_CHUNKS * CHUNK == 2 * ATT_BLOCK and B_MAX_REL <= ATT_BLOCK
    cos_t, sin_t = _rope_tables(s)
    row = lambda v: v.reshape(1, -1).astype(F32)
    for layer in range(depth):
        lam_init = 0.8 - 0.6 * math.exp(-0.3 * layer)
        km, vm, w_in_b = _mem_kv(mem, row(norm_mem_g[layer]), w_ckv[layer], w_in[layer])
        qkv, gates, wua, wub, wo, wcq, wco, wgu, wd = _in_proj(
            x, row(norm_mix_g[layer]), w_in_b, cos_t, sin_t,
            [w_up_a[layer], w_up_b[layer], w_out[layer], w_cq[layer], w_co[layer],
             w_gate_up[layer], w_down[layer]])
        ya, yb = _attn(qkv, row(lam_q1[layer]), row(lam_k1[layer]), row(lam_q2[layer]),
                       row(lam_k2[layer]), row(subln_g[layer]),
                       _band_bias_lines(rel_bias[layer]), lam_init)
        x = _post_mix(x, ya, yb, gates, km, vm, wua, wub, wo, row(norm_cross_g[layer]),
                      wcq, wco, row(norm_ffn_g[layer]), wgu, wd, row(norm_final_g),
                      final_norm=(layer == depth - 1))
    return x
```

```python
import functools
import math

import jax
import jax.numpy as jnp
from jax import lax
from jax.experimental import pallas as pl
from jax.experimental.pallas import tpu as pltpu

CHUNK = 64
EPS = 1e-6
ROPE_THETA = 10000.0
NEG = -1e30

A_HEADS = 4
A_DK = 64
A_DV = 2 * A_DK
B_HEADS = 8
B_DH = 64
B_LEFT_CHUNKS = 8
B_MAX_REL = 256
X_HEADS = 4

LANES_V7X = 128
BF16_TILE_ROWS_V7X = 16
VMEM_LIMIT_BYTES_V7X = 56 * 1024 * 1024

IN_TILE = 1024
SEQ_TILE = 512
ATT_BLOCK = 256
FF_CHUNK = 512

BF16 = jnp.bfloat16
F32 = jnp.float32
LOG2E = math.log2(math.e)


def _rms(x, g):
    return x * lax.rsqrt(jnp.mean(x * x, axis=-1, keepdims=True) + EPS) * g


def _dot(a, b):
    return jnp.dot(a, b, preferred_element_type=F32)


def _dot_nt(a, b):
    return lax.dot_general(a, b, (((1,), (1,)), ((), ())), preferred_element_type=F32)


def _sigmoid(x):
    return 1.0 / (1.0 + jnp.exp(-x))


def _const_spec(shape):
    n = len(shape)
    return pl.BlockSpec(shape, lambda *_: (0,) * n, pipeline_mode=pl.Buffered(1))


def _params(semantics):
    return pltpu.CompilerParams(dimension_semantics=semantics,
                                vmem_limit_bytes=VMEM_LIMIT_BYTES_V7X)


def _row_slices(w, n):
    rows, cols = w.shape
    assert rows % (n * BF16_TILE_ROWS_V7X) == 0
    return (rows // n, cols)


def _mem_kv_kernel(mem_ref, g_ref, w_ref, win_ref, k_ref, v_ref, winb_ref, wb_ref):
    nb, nm, d = mem_ref.shape

    @pl.when(pl.program_id(0) == 0)
    def _():
        wb_ref[...] = w_ref[...].astype(BF16)

    winb_ref[...] = win_ref[...].astype(BF16)
    m = mem_ref[...].reshape(nb * nm, d)
    mn = _rms(m, g_ref[...]).astype(BF16)
    kv = _dot(mn, wb_ref[...])
    k_ref[...] = kv[:, :d].astype(BF16).reshape(nb, nm, d)
    v_ref[...] = kv[:, d:].astype(BF16).reshape(nb, nm, d)


def _mem_kv(mem, g, w_ckv, w_in):
    b, nm, d = mem.shape
    nb = 2 if b % 2 == 0 else 1
    steps = b // nb
    win_blk = _row_slices(w_in, steps)
    return pl.pallas_call(
        _mem_kv_kernel,
        grid=(steps,),
        in_specs=[pl.BlockSpec((nb, nm, d), lambda i: (i, 0, 0)),
                  _const_spec((1, d)),
                  _const_spec((d, 2 * d)),
                  pl.BlockSpec(win_blk, lambda i: (i, 0))],
        out_specs=[pl.BlockSpec((nb, nm, d), lambda i: (i, 0, 0))] * 2
                  + [pl.BlockSpec(win_blk, lambda i: (i, 0))],
        out_shape=[jax.ShapeDtypeStruct((b, nm, d), BF16)] * 2
                  + [jax.ShapeDtypeStruct(w_in.shape, BF16)],
        scratch_shapes=[pltpu.VMEM((d, 2 * d), BF16)],
        compiler_params=_params(("arbitrary",)),
        name="mem_kv",
    )(mem, g, w_ckv, w_in)


N_SIDE_CASTS = 7


def _in_proj_kernel(*refs, q_scale):
    x_ref, g_ref, w_ref, cos_ref, sin_ref = refs[:5]
    w_f32 = refs[5:5 + N_SIDE_CASTS]
    qkv_ref, gate_ref = refs[5 + N_SIDE_CASTS:7 + N_SIDE_CASTS]
    w_bf16 = refs[7 + N_SIDE_CASTS:]

    @pl.when(pl.program_id(1) == 0)
    def _():
        for src, dst in zip(w_f32, w_bf16):
            dst[...] = src[...].astype(BF16)

    ts = x_ref.shape[1]
    h = _rms(x_ref[0], g_ref[...]).astype(BF16)
    cos = cos_ref[...]
    sin = sin_ref[...]
    lane = lax.broadcasted_iota(jnp.int32, (ts, LANES_V7X), 1)
    first_half = (lane % A_DK) < (A_DK // 2)
    slab_w = LANES_V7X
    grp_w = 4 * slab_w
    gate0 = 6 * grp_w
    for c in range(gate_ref.shape[2] // grp_w):
        z = _dot(h, w_ref[:, gate0 + c * grp_w: gate0 + (c + 1) * grp_w])
        gate_ref[0, :, c * grp_w:(c + 1) * grp_w] = _sigmoid(z).astype(BF16)
    for grp in (0, 1, 3, 2, 4, 5):
        z = _dot(h, w_ref[:, grp * grp_w:(grp + 1) * grp_w])
        for s in range(4):
            zs = z[:, s * slab_w:(s + 1) * slab_w]
            if grp < 2:
                partner = jnp.where(first_half,
                                    pltpu.roll(zs, LANES_V7X - A_DK // 2, 1),
                                    pltpu.roll(zs, A_DK // 2, 1))
                zs = zs * cos + partner * sin
            if grp in (0, 3):
                zs = zs * q_scale
            qkv_ref[0, grp * 4 + s] = zs.astype(BF16)


def _in_proj(x, g, w_in, cos_t, sin_t, weights):
    b, s, d = x.shape
    ts = IN_TILE
    n_gate = w_in.shape[1] - 6 * 4 * LANES_V7X
    assert A_DK == B_DH and len(weights) == N_SIDE_CASTS
    kern = functools.partial(_in_proj_kernel, q_scale=A_DK ** -0.5 * LOG2E)
    w_specs = [pl.BlockSpec(_row_slices(wt, b), lambda bi, i: (bi, 0)) for wt in weights]
    return pl.pallas_call(
        kern,
        grid=(b, s // ts),
        in_specs=[pl.BlockSpec((1, ts, d), lambda bi, i: (bi, i, 0)),
                  _const_spec((1, d)),
                  _const_spec(w_in.shape),
                  pl.BlockSpec((ts, LANES_V7X), lambda bi, i: (i, 0)),
                  pl.BlockSpec((ts, LANES_V7X), lambda bi, i: (i, 0))] + w_specs,
        out_specs=[pl.BlockSpec((1, 24, ts, LANES_V7X), lambda bi, i: (bi, 0, i, 0)),
                   pl.BlockSpec((1, ts, n_gate), lambda bi, i: (bi, i, 0))] + w_specs,
        out_shape=[jax.ShapeDtypeStruct((b, 24, s, LANES_V7X), BF16),
                   jax.ShapeDtypeStruct((b, s, n_gate), BF16)]
                  + [jax.ShapeDtypeStruct(wt.shape, BF16) for wt in weights],
        compiler_params=_params(("arbitrary", "arbitrary")),
        name="in_proj",
    )(x, g, w_in, cos_t, sin_t, *weights)


def _stack_halves(q, split):
    lane = lax.broadcasted_iota(jnp.int32, q.shape, 1)
    zero = jnp.zeros_like(q)
    return jnp.concatenate([jnp.where(lane < split, q, zero),
                            jnp.where(lane >= split, q, zero)], axis=0)


ONES_ROWS = BF16_TILE_ROWS_V7X


def _load_vt(v_ref, vt_ref):
    w, seq = v_ref.shape[3], v_ref.shape[2]
    vt_ref[:w, :] = v_ref[0, 0].T
    vt_ref[w:, :] = jnp.ones((ONES_ROWS, seq), BF16)


def _row_softmax_pv(s, v):
    p = jnp.exp(s - jnp.max(s, axis=1, keepdims=True))
    return _dot(p.astype(BF16), v), jnp.sum(p, axis=1, keepdims=True)


def _diff_attn_kernel(q_ref, k_ref, v_ref, lq1_ref, lk1_ref, lq2_ref, lk2_ref, g_ref,
                      o_ref, vt_ref, s0_ref, s1_ref, p0_ref, p1_ref, *, lam_init):
    tq = ATT_BLOCK
    seq, w = q_ref.shape[2], q_ref.shape[3]
    _load_vt(v_ref, vt_ref)
    lam = (jnp.exp(jnp.sum(lq1_ref[...] * lk1_ref[...], axis=1, keepdims=True))
           - jnp.exp(jnp.sum(lq2_ref[...] * lk2_ref[...], axis=1, keepdims=True))
           + lam_init)
    key = lax.broadcasted_iota(jnp.int32, (tq, 2 * tq), 0)
    qry = lax.broadcasted_iota(jnp.int32, (tq, 2 * tq), 1)
    allowed = (key // CHUNK) <= ((qry % tq) // CHUNK)

    s_refs, p_refs = (s0_ref, s1_ref), (p0_ref, p1_ref)

    def scores(slot, qi):
        qq = _stack_halves(q_ref[0, 0, qi * tq:(qi + 1) * tq, :], A_DK)
        m8 = None
        for kj in range(qi + 1):
            rows = slice(kj * tq, (kj + 1) * tq)
            s = _dot_nt(k_ref[0, 0, rows, :], qq)
            if kj == qi:
                s = jnp.where(allowed, s, NEG)
            s_refs[slot][rows, :] = s
            mj = jnp.max(s.reshape(tq // 8, 8, 2 * tq), axis=0)
            m8 = mj if m8 is None else jnp.maximum(m8, mj)
        return jnp.max(m8, axis=0, keepdims=True)

    def probs(slot, qi, m):
        for kj in range(qi + 1):
            rows = slice(kj * tq, (kj + 1) * tq)
            p = jnp.exp2(s_refs[slot][rows, :] - m)
            p_refs[slot][rows, :] = p.astype(BF16)

    def values(slot, qi):
        q0, q1 = qi * tq, (qi + 1) * tq
        ot = _dot(vt_ref[:, :q1], p_refs[slot][:q1, :])
        ot = ot[:w] / ot[w:w + 1]
        o = (ot[:, :tq] - lam * ot[:, tq:]).T
        o = _rms(o, g_ref[...]) * (1.0 - lam_init)
        o_ref[0, q0:q1, :] = o.astype(BF16)

    nq = seq // tq
    order = list(range(nq))
    m_cur = scores(0, order[0])
    for pos, qi in enumerate(order):
        m_next = scores((pos + 1) % 2, order[pos + 1]) if pos + 1 < nq else None
        if pos:
            values((pos - 1) % 2, order[pos - 1])
        probs(pos % 2, qi, m_cur)
        m_cur = m_next
    values((nq - 1) % 2, order[-1])


def _band_tiles_init(bias_ref, tab_ref):
    t = ATT_BLOCK
    kc = lax.broadcasted_iota(jnp.int32, (t, t), 0) // CHUNK
    qc = lax.broadcasted_iota(jnp.int32, (t, t), 1) // CHUNK
    for head in range(2):
        cols = slice(head * t, (head + 1) * t)
        for back in range(2):
            line = jnp.broadcast_to(bias_ref[0, head, back:back + 1, :], (t, 2 * t))
            tile = pltpu.roll(line, 0, 1, stride=1, stride_axis=0)[:, :t]
            tab_ref[back, :, cols] = jnp.where(kc <= qc, tile, NEG) if back == 0 else tile
        far = jnp.broadcast_to(bias_ref[0, head, 1:2, 0:1], (t, t))
        tab_ref[2, :, cols] = jnp.where(kc >= qc, far, NEG)


def _band_attn_kernel(q_ref, k_ref, v_ref, bias_ref, o_ref, tab_ref, vt_ref,
                      s0_ref, s1_ref, p0_ref, p1_ref):
    tq = ATT_BLOCK
    seq, w = q_ref.shape[2], q_ref.shape[3]
    _load_vt(v_ref, vt_ref)
    first_head = lax.broadcasted_iota(jnp.int32, (LANES_V7X, tq), 0) < B_DH
    s_refs, p_refs = (s0_ref, s1_ref), (p0_ref, p1_ref)

    def scores(qi):
        qq = _stack_halves(q_ref[0, 0, qi * tq:(qi + 1) * tq, :], B_DH)
        m8 = None
        for back in range(min(qi, 2) + 1):
            kj = qi - back
            s = _dot_nt(k_ref[0, 0, kj * tq:(kj + 1) * tq, :], qq) + tab_ref[back]
            s_refs[qi % 2][back * tq:(back + 1) * tq, :] = s
            mj = jnp.max(s.reshape(tq // 8, 8, 2 * tq), axis=0)
            m8 = mj if m8 is None else jnp.maximum(m8, mj)
        return jnp.max(m8, axis=0, keepdims=True)

    def probs(qi, m):
        nblk = min(qi, 2) + 1
        for back in range(nblk):
            p = jnp.exp2(s_refs[qi % 2][back * tq:(back + 1) * tq, :] - m)
            pos = nblk - 1 - back
            p_refs[qi % 2][pos * tq:(pos + 1) * tq, :] = p.astype(BF16)

    def values(qi):
        q0, q1 = qi * tq, (qi + 1) * tq
        nblk = min(qi, 2) + 1
        ot = _dot(vt_ref[:, q1 - nblk * tq:q1], p_refs[qi % 2][:nblk * tq, :])
        ot = ot[:w] / ot[w:w + 1]
        o = jnp.where(first_head, ot[:, :tq], ot[:, tq:]).T
        o_ref[0, q0:q1, :] = o.astype(BF16)

    nq = seq // tq
    m_cur = scores(0)
    for qi in range(nq):
        m_next = scores(qi + 1) if qi + 1 < nq else None
        probs(qi, m_cur)
        values(qi)
        m_cur = m_next


def _band_bias_lines(rel_bias):
    t = ATT_BLOCK
    u = jnp.arange(2 * t)
    u = jnp.where(u < t, u, u - 2 * t)
    idx = jnp.stack([jnp.clip(u, -B_MAX_REL, B_MAX_REL), jnp.clip(t + u, -B_MAX_REL, B_MAX_REL)])
    lines = rel_bias[:, idx + B_MAX_REL].astype(F32) * LOG2E
    return lines.reshape(B_HEADS // 2, 2, 2, 2 * t)


def _attn_kernel(qa_ref, ka_ref, va_ref, lq1_ref, lk1_ref, lq2_ref, lk2_ref, g_ref,
                 qb_ref, kb_ref, vb_ref, bias_ref, ya_ref, yb_ref,
                 vta_ref, sa0_ref, sa1_ref, pa0_ref, pa1_ref,
                 tab_ref, vtb_ref, sb0_ref, sb1_ref, pb0_ref, pb1_ref, *, lam_init):
    pair_tab = tab_ref.at[pl.program_id(1)]

    @pl.when(pl.program_id(0) == 0)
    def _():
        _band_tiles_init(bias_ref, pair_tab)

    _diff_attn_kernel(qa_ref, ka_ref, va_ref, lq1_ref, lk1_ref, lq2_ref, lk2_ref, g_ref,
                      ya_ref, vta_ref, sa0_ref, sa1_ref, pa0_ref, pa1_ref, lam_init=lam_init)
    _band_attn_kernel(qb_ref, kb_ref, vb_ref, bias_ref, yb_ref, pair_tab, vtb_ref,
                      sb0_ref, sb1_ref, pb0_ref, pb1_ref)


def _attn(qkv, lq1, lk1, lq2, lk2, subln_g, lines, lam_init):
    b, _, s, w = qkv.shape
    tq = ATT_BLOCK
    npair = B_HEADS // 2
    assert npair == A_HEADS
    kern = functools.partial(_attn_kernel, lam_init=lam_init)
    vec = _const_spec((1, A_DK))
    slab = lambda g0: pl.BlockSpec((1, 1, s, w), lambda bi, h: (bi, g0 + h, 0, 0))
    out = pl.BlockSpec((1, s, w), lambda bi, h: (bi, 0, h))
    return pl.pallas_call(
        kern,
        grid=(b, A_HEADS),
        in_specs=[slab(0), slab(4), slab(8), vec, vec, vec, vec, _const_spec((1, A_DV)),
                  slab(12), slab(16), slab(20),
                  pl.BlockSpec((1, 2, 2, 2 * tq), lambda bi, h: (h, 0, 0, 0))],
        out_specs=[out, out],
        out_shape=[jax.ShapeDtypeStruct((b, s, A_HEADS * A_DV), BF16),
                   jax.ShapeDtypeStruct((b, s, B_HEADS * B_DH), BF16)],
        scratch_shapes=[pltpu.VMEM((w + ONES_ROWS, s), BF16),
                        pltpu.VMEM((s, 2 * tq), F32), pltpu.VMEM((s, 2 * tq), F32),
                        pltpu.VMEM((s, 2 * tq), BF16), pltpu.VMEM((s, 2 * tq), BF16),
                        pltpu.VMEM((npair, 3, tq, 2 * tq), F32),
                        pltpu.VMEM((w + ONES_ROWS, s), BF16),
                        pltpu.VMEM((3 * tq, 2 * tq), F32), pltpu.VMEM((3 * tq, 2 * tq), F32),
                        pltpu.VMEM((3 * tq, 2 * tq), BF16), pltpu.VMEM((3 * tq, 2 * tq), BF16)],
        compiler_params=_params(("arbitrary", "arbitrary")),
        name="attn",
    )(qkv, qkv, qkv, lq1, lk1, lq2, lk2, subln_g, qkv, qkv, qkv, lines)


def _post_mix_kernel(x_ref, ya_ref, yb_ref, gate_ref, km_ref, vm_ref,
                     wua_ref, wub_ref, wo_ref, gc_ref, wcq_ref, wco_ref,
                     gf_ref, wgu_ref, wd_ref, gfin_ref, o_ref, *, final_norm):
    d = x_ref.shape[2]
    ua = _dot(ya_ref[0], wua_ref[...])
    ub = _dot(yb_ref[0], wub_ref[...])
    merged = (gate_ref[0, :, :d].astype(F32) * ua + gate_ref[0, :, d:].astype(F32) * ub)
    x = x_ref[0] + _dot(merged.astype(BF16), wo_ref[...])

    dh = d // X_HEADS
    hc = _rms(x, gc_ref[...]).astype(BF16)
    q = (_dot(hc, wcq_ref[...]) * (dh ** -0.5)).astype(BF16)
    heads = []
    for h in range(X_HEADS):
        sl = slice(h * dh, (h + 1) * dh)
        pv, l = _row_softmax_pv(_dot_nt(q[:, sl], km_ref[0, :, sl]), vm_ref[0, :, sl])
        heads.append((pv / l).astype(BF16))
    x = x + _dot(jnp.concatenate(heads, axis=1), wco_ref[...])

    hf = _rms(x, gf_ref[...]).astype(BF16)
    dff = wd_ref.shape[0]
    acc = jnp.zeros_like(x)
    for c0 in range(0, dff, FF_CHUNK):
        c1 = min(c0 + FF_CHUNK, dff)
        g = _dot(hf, wgu_ref[:, c0:c1])
        u = _dot(hf, wgu_ref[:, dff + c0:dff + c1])
        a = (g * _sigmoid(g) * u).astype(BF16)
        acc = acc + _dot(a, wd_ref[c0:c1, :])
    x = x + acc
    if final_norm:
        x = _rms(x, gfin_ref[...])
    o_ref[0] = x


def _post_mix(x, ya, yb, gates, km, vm, wua, wub, wo, gc, wcq, wco, gf, wgu, wd, gfin,
              final_norm):
    b, s, d = x.shape
    ts = SEQ_TILE
    nm = km.shape[1]
    row = lambda w: pl.BlockSpec((1, ts, w), lambda bi, i: (bi, i, 0))
    mem = pl.BlockSpec((1, nm, d), lambda bi, i: (bi, 0, 0))
    kern = functools.partial(_post_mix_kernel, final_norm=final_norm)
    return pl.pallas_call(
        kern,
        grid=(b, s // ts),
        in_specs=[row(d), row(ya.shape[2]), row(yb.shape[2]), row(gates.shape[2]), mem, mem,
                  _const_spec(wua.shape), _const_spec(wub.shape), _const_spec(wo.shape),
                  _const_spec((1, d)), _const_spec(wcq.shape), _const_spec(wco.shape),
                  _const_spec((1, d)), _const_spec(wgu.shape), _const_spec(wd.shape),
                  _const_spec((1, d))],
        out_specs=row(d),
        out_shape=jax.ShapeDtypeStruct((b, s, d), F32),
        compiler_params=_params(("arbitrary", "arbitrary")),
        name="post_mix",
    )(x, ya, yb, gates, km, vm, wua, wub, wo, gc, wcq, wco, gf, wgu, wd, gfin)


def _rope_tables(seq):
    inv = 1.0 / (ROPE_THETA ** (jnp.arange(0, A_DK, 2, dtype=F32) / A_DK))
    ang = jnp.arange(seq, dtype=F32)[:, None] * inv[None, :]
    cos, sin = jnp.cos(ang), jnp.sin(ang)
    reps = LANES_V7X // A_DK
    cos_t = jnp.tile(jnp.concatenate([cos, cos], axis=1), (1, reps))
    sin_t = jnp.tile(jnp.concatenate([-sin, sin], axis=1), (1, reps))
    return cos_t, sin_t


def kernel(x, mem, norm_mix_g, w_in, lam_q1, lam_k1, lam_q2, lam_k2, subln_g, rel_bias, w_up_a, w_up_b, w_out, norm_cross_g, norm_mem_g, w_cq, w_ckv, w_co, norm_ffn_g, w_gate_up, w_down, norm_final_g):
    b, s, d = x.shape
    depth = w_in.shape[0]
    assert s % SEQ_TILE == 0 and s % IN_TILE == 0 and s % ATT_BLOCK == 0 and ATT_BLOCK == 4 * CHUNK
    assert B_LEFT_CHUNKS * CHUNK == 2 * ATT_BLOCK and B_MAX_REL <= ATT_BLOCK
    cos_t, sin_t = _rope_tables(s)
    row = lambda v: v.reshape(1, -1).astype(F32)
    for layer in range(depth):
        lam_init = 0.8 - 0.6 * math.exp(-0.3 * layer)
        km, vm, w_in_b = _mem_kv(mem, row(norm_mem_g[layer]), w_ckv[layer], w_in[layer])
        qkv, gates, wua, wub, wo, wcq, wco, wgu, wd = _in_proj(
            x, row(norm_mix_g[layer]), w_in_b, cos_t, sin_t,
            [w_up_a[layer], w_up_b[layer], w_out[layer], w_cq[layer], w_co[layer],
             w_gate_up[layer], w_down[layer]])
        ya, yb = _attn(qkv, row(lam_q1[layer]), row(lam_k1[layer]), row(lam_q2[layer]),
                       row(lam_k2[layer]), row(subln_g[layer]),
                       _band_bias_lines(rel_bias[layer]), lam_init)
        x = _post_mix(x, ya, yb, gates, km, vm, wua, wub, wo, row(norm_cross_g[layer]),
                      wcq, wco, row(norm_ffn_g[layer]), wgu, wd, row(norm_final_g),
                      final_norm=(layer == depth - 1))
    return x
```
